```python
import functools
import jax, jax.numpy as jnp
from jax import lax
import numpy as np

D_MODEL = 1024
BATCH = 8
SEQ = 4096
DEPTH = 1
DEC_BATCH = 128
DEC_SEQ = 4
PAST_LEN = 8192
PAGE_SIZE = 128

N_HEADS = 8
HEAD_DIM = 64
D_ATTN = N_HEADS * HEAD_DIM
SB_SCALE = HEAD_DIM ** -0.5
SB_BIAS_HI = -5.0
SB_BIAS_LO = -8.0
Q_BLOCK = 128
D_CONV = D_MODEL // 2
CONV_WIDTH = 3
SPLIT_POINTS = (D_ATTN, 2 * D_ATTN, 3 * D_ATTN, 3 * D_ATTN + D_CONV, 3 * D_ATTN + 2 * D_CONV,
                3 * D_ATTN + 3 * D_CONV, 3 * D_ATTN + 3 * D_CONV + D_MODEL)
D_IN = 3 * D_ATTN + 3 * D_CONV + 2 * D_MODEL
N_EXPERTS = 32
TOP_K = 4
D_EXPERT = D_MODEL
SWIGLU_LIMIT = 7.0
SWIGLU_ALPHA = 1.702
MOE_BLOCK = 256
LN_EPS = 1e-5
DN_ALPHA = (2 * DEPTH) ** 0.25
DN_BETA = (8 * DEPTH) ** -0.25

kernel_name = 'stickbreak_shortconv_moe_hybrid_step'


def layer_norm(x, g=None, b=None):
    xf = x.astype(jnp.float32)
    mu = jnp.mean(xf, axis=-1, keepdims=True)
    var = jnp.mean(jnp.square(xf - mu), axis=-1, keepdims=True)
    y = (xf - mu) * lax.rsqrt(var + LN_EPS)
    if g is not None:
        y = y * g.astype(jnp.float32) + b.astype(jnp.float32)
    return y.astype(x.dtype)


def sb_block(z, mask, carry, bias):
    z = z + bias.astype(jnp.float32)[:, None, None]
    log_rem = jnp.where(mask, jax.nn.log_sigmoid(-z), 0.0)
    suffix = lax.cumsum(log_rem, axis=z.ndim - 1, reverse=True) - log_rem + carry[..., None]
    a = jnp.where(mask, jnp.exp(jax.nn.log_sigmoid(z) + suffix), 0.0)
    return a, carry + jnp.sum(log_rem, axis=-1)


def sb_attention_prompt(q, k, v, sb_bias):
    bsz, s_len = q.shape[:2]
    nb = s_len // Q_BLOCK
    qb = (q.astype(jnp.float32) * SB_SCALE).reshape(bsz, nb, Q_BLOCK, N_HEADS, HEAD_DIM).swapaxes(0, 1)
    kf = k.astype(jnp.float32)
    vf = v.astype(jnp.float32)
    kpos = jnp.arange(s_len)
    carry0 = jnp.zeros((bsz, N_HEADS, Q_BLOCK), jnp.float32)

    def q_block(args):
        qi, i = args
        qpos = i * Q_BLOCK + jnp.arange(Q_BLOCK)
        z = jnp.einsum('bqhd,bkhd->bhqk', qi, kf)
        a, _ = sb_block(z, kpos[None, :] < qpos[:, None], carry0, sb_bias)
        return jnp.einsum('bhqk,bkhd->bqhd', a, vf)

    o = lax.map(q_block, (qb, jnp.arange(nb)))
    return o.swapaxes(0, 1).reshape(bsz, s_len, D_ATTN).astype(q.dtype)


def sb_attention_sample(q, k, v, sb_bias, cache_k, cache_v, page_table, layer):
    bsz, t_new = q.shape[:2]
    qf = q.astype(jnp.float32) * SB_SCALE
    kf = k.astype(jnp.float32)
    vf = v.astype(jnp.float32)
    causal_new = jnp.arange(t_new)[None, :] < jnp.arange(t_new)[:, None]
    z_new = jnp.einsum('bqhd,bkhd->bhqk', qf, kf)
    a_new, carry = sb_block(z_new, causal_new, jnp.zeros((bsz, N_HEADS, t_new), jnp.float32), sb_bias)
    o = jnp.einsum('bhqk,bkhd->bqhd', a_new, vf)
    past_visible = jnp.ones((t_new, cache_k.shape[2]), bool)

    def page_step(state, p):
        o_acc, carry_acc = state
        phys = page_table[:, p]
        kp = cache_k[layer, phys].astype(jnp.float32)
        vp = cache_v[layer, phys].astype(jnp.float32)
        z = jnp.einsum('bqhd,bkhd->bhqk', qf, kp)
        a, carry_acc = sb_block(z, past_visible, carry_acc, sb_bias)
        return (o_acc + jnp.einsum('bhqk,bkhd->bqhd', a, vp), carry_acc), None

    (o, _), _ = lax.scan(page_step, (o, carry), jnp.arange(page_table.shape[1]), reverse=True)
    return o.reshape(bsz, t_new, D_ATTN).astype(q.dtype)


def short_conv(z, prev, w):
    zp = jnp.concatenate([prev, z], axis=1)
    t_len = z.shape[1]
    out = w[0] * zp[:, 0:t_len]
    for i in range(1, CONV_WIDTH):
        out = out + w[i] * zp[:, i:i + t_len]
    return out, zp[:, zp.shape[1] - (CONV_WIDTH - 1):]


def moe(h, w_router, b_router, w_gate, b_gate, w_up, b_up, w_down, b_down):
    lead = h.shape[:-1]
    t = h.reshape(-1, D_MODEL)
    n_tok = t.shape[0]
    n_assign = n_tok * TOP_K
    logits = (t @ w_router + b_router).astype(jnp.float32)
    top_val, top_idx = lax.top_k(logits, TOP_K)
    probs = jax.nn.softmax(top_val, axis=-1)
    flat_e = top_idx.reshape(-1)
    flat_tok = jnp.arange(n_assign) // TOP_K
    order = jnp.argsort(flat_e)
    se = flat_e[order]
    stok = flat_tok[order]
    sp = probs.reshape(-1)[order]
    counts = jnp.bincount(flat_e, length=N_EXPERTS)
    padded = (counts + MOE_BLOCK - 1) // MOE_BLOCK * MOE_BLOCK
    pad_end = jnp.cumsum(padded)
    pad_start = pad_end - padded
    start = jnp.cumsum(counts) - counts
    dest = pad_start[se] + jnp.arange(n_assign) - start[se]
    n_blocks = (n_assign + MOE_BLOCK - 1) // MOE_BLOCK + N_EXPERTS
    rows = jnp.full((n_blocks * MOE_BLOCK,), n_tok, jnp.int32).at[dest].set(stok)
    block_e = jnp.minimum(jnp.searchsorted(pad_end, jnp.arange(n_blocks) * MOE_BLOCK, side='right'),
                          N_EXPERTS - 1)
    t_pad = jnp.concatenate([t, jnp.zeros((1, D_MODEL), t.dtype)], axis=0)
    xb = t_pad[rows].reshape(n_blocks, MOE_BLOCK, D_MODEL)

    def expert_block(args):
        xe, e = args
        g = jnp.minimum(xe @ w_gate[e] + b_gate[e], SWIGLU_LIMIT)
        u = jnp.clip(xe @ w_up[e] + b_up[e], -SWIGLU_LIMIT, SWIGLU_LIMIT)
        act = g * jax.nn.sigmoid(SWIGLU_ALPHA * g) * (u + 1.0)
        return act @ w_down[e] + b_down[e]

    yb = lax.map(expert_block, (xb, block_e)).reshape(-1, D_MODEL)
    contrib = yb[dest] * sp[:, None].astype(yb.dtype)
    out = jax.ops.segment_sum(contrib, stok, num_segments=n_tok)
    return out.reshape(*lead, D_MODEL)


def decoder_layer(x, c, attend, conv_prev, w_ada, b_ada, w_in, sb_bias, w_conv, w_pa, w_pb, w_o, ln1_g, ln1_b,
                  w_router, b_router, w_gate, b_gate, w_up, b_up, w_down, b_down, ln2_g, ln2_b):
    bsz, t_len = x.shape[:2]
    shift1, scale1, gate1, shift2, scale2, gate2 = jnp.split((c @ w_ada + b_ada)[:, None, :], 6, axis=-1)
    h = layer_norm(x) * (1.0 + scale1) + shift1
    q, k, v, u, g_b, g_c, gate_a, gate_bb = jnp.split(h @ w_in, SPLIT_POINTS, axis=-1)
    q = q.reshape(bsz, t_len, N_HEADS, HEAD_DIM)
    k = k.reshape(bsz, t_len, N_HEADS, HEAD_DIM)
    v = v.reshape(bsz, t_len, N_HEADS, HEAD_DIM)
    attn = attend(q, k, v, sb_bias)
    conv_out, conv_tail = short_conv(g_c * u, conv_prev, w_conv)
    branch_a = attn @ w_pa
    branch_b = (g_b * conv_out) @ w_pb
    merged = jax.nn.sigmoid(gate_a) * branch_a + jax.nn.sigmoid(gate_bb) * branch_b
    x = layer_norm(DN_ALPHA * x + (1.0 + gate1) * (merged @ w_o), ln1_g, ln1_b)
    h2 = layer_norm(x) * (1.0 + scale2) + shift2
    y2 = moe(h2, w_router, b_router, w_gate, b_gate, w_up, b_up, w_down, b_down)
    x = layer_norm(DN_ALPHA * x + (1.0 + gate2) * y2, ln2_g, ln2_b)
    return x, k, v, conv_tail


def setup_inputs(seed: int = 0) -> dict:
    key = jax.random.key(seed)
    ks = jax.random.split(key, 32)
    f32 = jnp.float32
    n_pages = PAST_LEN // PAGE_SIZE
    n_pool = (DEC_BATCH * n_pages * 5) // 4

    def nrm(k, shape, scale=1.0):
        return jax.random.normal(k, shape, f32) * scale

    page_table = jax.random.permutation(ks[5], n_pool)[: DEC_BATCH * n_pages].reshape(
        DEC_BATCH, n_pages).astype(jnp.int32)
    sb_bias = (jnp.linspace(SB_BIAS_HI, SB_BIAS_LO, N_HEADS, dtype=f32)[None, :]
               + nrm(ks[27], (DEPTH, N_HEADS), 0.1))
    return {
        'x_prompt': nrm(ks[0], (BATCH, SEQ, D_MODEL)),
        'x_sample': nrm(ks[1], (DEC_BATCH, DEC_SEQ, D_MODEL)),
        'cache_k': nrm(ks[2], (DEPTH, n_pool, PAGE_SIZE, N_HEADS, HEAD_DIM)),
        'cache_v': nrm(ks[3], (DEPTH, n_pool, PAGE_SIZE, N_HEADS, HEAD_DIM)),
        'state_conv': nrm(ks[4], (DEPTH, DEC_BATCH, CONV_WIDTH - 1, D_CONV)),
        'page_table': page_table,
        'c_prompt': nrm(ks[6], (BATCH, D_MODEL)),
        'c_sample': nrm(ks[7], (DEC_BATCH, D_MODEL)),
        'w_ada': nrm(ks[8], (DEPTH, D_MODEL, 6 * D_MODEL), 0.1 * D_MODEL ** -0.5),
        'b_ada': nrm(ks[9], (DEPTH, 6 * D_MODEL), 0.01),
        'w_in': nrm(ks[10], (DEPTH, D_MODEL, D_IN), D_MODEL ** -0.5),
        'sb_bias': sb_bias,
        'w_conv': nrm(ks[11], (DEPTH, CONV_WIDTH, D_CONV), CONV_WIDTH ** -0.5),
        'w_pa': nrm(ks[12], (DEPTH, D_ATTN, D_MODEL), DN_BETA * D_ATTN ** -0.5),
        'w_pb': nrm(ks[13], (DEPTH, D_CONV, D_MODEL), DN_BETA * D_CONV ** -0.5),
        'w_o': nrm(ks[14], (DEPTH, D_MODEL, D_MODEL), DN_BETA * D_MODEL ** -0.5),
        'ln1_g': 1.0 + nrm(ks[15], (DEPTH, D_MODEL), 0.02),
        'ln1_b': nrm(ks[16], (DEPTH, D_MODEL), 0.02),
        'w_router': nrm(ks[17], (DEPTH, D_MODEL, N_EXPERTS), D_MODEL ** -0.5),
        'b_router': nrm(ks[18], (DEPTH, N_EXPERTS), 0.01),
        'w_gate': nrm(ks[19], (DEPTH, N_EXPERTS, D_MODEL, D_EXPERT), D_MODEL ** -0.5),
        'b_gate': nrm(ks[20], (DEPTH, N_EXPERTS, D_EXPERT), 0.02),
        'w_up': nrm(ks[21], (DEPTH, N_EXPERTS, D_MODEL, D_EXPERT), D_MODEL ** -0.5),
        'b_up': nrm(ks[22], (DEPTH, N_EXPERTS, D_EXPERT), 0.02),
        'w_down': nrm(ks[23], (DEPTH, N_EXPERTS, D_EXPERT, D_MODEL), DN_BETA * D_EXPERT ** -0.5),
        'b_down': nrm(ks[24], (DEPTH, N_EXPERTS, D_MODEL), 0.02),
        'ln2_g': 1.0 + nrm(ks[25], (DEPTH, D_MODEL), 0.02),
        'ln2_b': nrm(ks[26], (DEPTH, D_MODEL), 0.02),
    }


def reference(x_prompt, x_sample, cache_k, cache_v, state_conv, page_table, c_prompt, c_sample,
              w_ada, b_ada, w_in, sb_bias, w_conv, w_pa, w_pb, w_o, ln1_g, ln1_b,
              w_router, b_router, w_gate, b_gate, w_up, b_up, w_down, b_down, ln2_g, ln2_b):
    y_p = x_prompt
    y_s = x_sample
    k_p_list, v_p_list, conv_p_list = [], [], []
    k_s_list, v_s_list, conv_s_list = [], [], []
    for l in range(DEPTH):
        lw = (w_ada[l], b_ada[l], w_in[l], sb_bias[l], w_conv[l], w_pa[l], w_pb[l], w_o[l], ln1_g[l], ln1_b[l],
              w_router[l], b_router[l], w_gate[l], b_gate[l], w_up[l], b_up[l], w_down[l], b_down[l],
              ln2_g[l], ln2_b[l])
        conv_zero = jnp.zeros((y_p.shape[0], CONV_WIDTH - 1, D_CONV), y_p.dtype)
        y_p, k_p, v_p, conv_p = decoder_layer(y_p, c_prompt, sb_attention_prompt, conv_zero, *lw)
        attend_s = functools.partial(sb_attention_sample, cache_k=cache_k, cache_v=cache_v,
                                     page_table=page_table, layer=l)
        y_s, k_s, v_s, conv_s = decoder_layer(y_s, c_sample, attend_s, state_conv[l], *lw)
        k_p_list.append(k_p)
        v_p_list.append(v_p)
        conv_p_list.append(conv_p)
        k_s_list.append(k_s)
        v_s_list.append(v_s)
        conv_s_list.append(conv_s)
    k_prompt = jnp.stack(k_p_list)
    v_prompt = jnp.stack(v_p_list)
    conv_prompt = jnp.stack(conv_p_list)
    k_sample = jnp.stack(k_s_list)
    v_sample = jnp.stack(v_s_list)
    conv_sample = jnp.stack(conv_s_list)
    return (y_p, y_s, k_prompt, v_prompt, conv_prompt, k_sample, v_sample, conv_sample)
```

```python
import functools

import jax
import jax.numpy as jnp
from jax import lax
from jax.experimental import pallas as pl
from jax.experimental.pallas import tpu as pltpu

F32 = jnp.float32
BF16 = jnp.bfloat16

HEAD_DIM = 64
TOP_K = 4
MOE_BLOCK = 256
SWIGLU_LIMIT = 7.0
SWIGLU_ALPHA = 1.702
LN_EPS = 1e-5

LANES = 128
SUBLANES = 8
HEADS_PER_GROUP = LANES // HEAD_DIM
VMEM_LIMIT = 56 * 1024 * 1024

_NT = (((1,), (1,)), ((), ()))


def _params(*sem):
    return pltpu.CompilerParams(dimension_semantics=sem, vmem_limit_bytes=VMEM_LIMIT)


def _layer_norm(x):
    mu = jnp.mean(x, axis=-1, keepdims=True)
    xc = x - mu
    var = jnp.mean(xc * xc, axis=-1, keepdims=True)
    return xc * lax.rsqrt(var + LN_EPS)


def _softplus(z):
    return jnp.maximum(z, 0.0) + jnp.log(1.0 + jnp.exp(-jnp.abs(z)))


def _const_spec(shape):
    return pl.BlockSpec(shape, lambda *_: (0,) * len(shape))


def _ada_kernel(c_ref, w_ref, b_ref, o_ref):
    o_ref[...] = jnp.dot(c_ref[...], w_ref[...], preferred_element_type=F32,
                         precision=lax.Precision.HIGHEST) + b_ref[...]


def _ada(c, w, b):
    m, d = c.shape
    n = w.shape[1]
    tn = n // 6
    return pl.pallas_call(
        _ada_kernel,
        grid=(n // tn,),
        in_specs=[_const_spec((m, d)),
                  pl.BlockSpec((d, tn), lambda j: (0, j)),
                  pl.BlockSpec((1, tn), lambda j: (0, j))],
        out_specs=pl.BlockSpec((m, tn), lambda j: (0, j)),
        out_shape=jax.ShapeDtypeStruct((m, n), F32),
        compiler_params=_params("arbitrary"),
        name="ada",
    )(c, w, b.reshape(1, n))


def _qkv_prompt_kernel(x_ref, shift_ref, scale_ref, wq_ref, wkt_ref, wvt_ref, wv_ref,
                       q_ref, kt_ref, ktb_ref, vt_ref, vb_ref, *, q_scale):
    h = (_layer_norm(x_ref[...]) * (1.0 + scale_ref[0]) + shift_ref[0]).astype(BF16)
    q = jnp.dot(h, wq_ref[...], preferred_element_type=F32)
    q_ref[...] = (q * q_scale).astype(BF16)
    kt = lax.dot_general(wkt_ref[...], h, _NT, preferred_element_type=F32)
    kt_ref[0] = kt
    ktb_ref[0] = kt.astype(BF16)
    vt_ref[0] = lax.dot_general(wvt_ref[...], h, _NT, preferred_element_type=F32)
    vb_ref[...] = jnp.dot(h, wv_ref[...], preferred_element_type=F32).astype(BF16)


def _qkv_prompt(x, shift, scale, wq, wkt, wvt, wv, n_seq, tm):
    n, d = x.shape
    da = wq.shape[1]
    s_len = n // n_seq
    tps = s_len // tm
    mod = pl.BlockSpec((1, 1, d), lambda i: (i // tps, 0, 0))
    row = pl.BlockSpec((tm, da), lambda i: (i, 0))
    tr = pl.BlockSpec((1, da, tm), lambda i: (i // tps, 0, i % tps))
    return pl.pallas_call(
        functools.partial(_qkv_prompt_kernel, q_scale=HEAD_DIM ** -0.5),
        grid=(n // tm,),
        in_specs=[pl.BlockSpec((tm, d), lambda i: (i, 0)), mod, mod,
                  _const_spec((d, da)), _const_spec((da, d)), _const_spec((da, d)), _const_spec((d, da))],
        out_specs=[row, tr, tr, tr, row],
        out_shape=[jax.ShapeDtypeStruct((n, da), BF16),
                   jax.ShapeDtypeStruct((n_seq, da, s_len), F32),
                   jax.ShapeDtypeStruct((n_seq, da, s_len), BF16),
                   jax.ShapeDtypeStruct((n_seq, da, s_len), F32),
                   jax.ShapeDtypeStruct((n, da), BF16)],
        compiler_params=_params("arbitrary"),
        name="qkv_prompt",
    )(x, shift, scale, wq, wkt, wvt, wv)


def _qkv_sample_kernel(x_ref, shift_ref, scale_ref, wq_ref, wk_ref, wv_ref, q_ref, k_ref, v_ref, *, q_scale):
    h = (_layer_norm(x_ref[...]) * (1.0 + scale_ref[0]) + shift_ref[0]).astype(BF16)
    q_ref[...] = (jnp.dot(h, wq_ref[...], preferred_element_type=F32) * q_scale).astype(BF16)
    k_ref[...] = jnp.dot(h, wk_ref[...], preferred_element_type=F32)
    v_ref[...] = jnp.dot(h, wv_ref[...], preferred_element_type=F32)


def _qkv_sample(x, shift, scale, wq, wk, wv, tm):
    n, d = x.shape
    da = wq.shape[1]
    mod = pl.BlockSpec((1, tm, d), lambda i: (i, 0, 0))
    row = pl.BlockSpec((tm, da), lambda i: (i, 0))
    return pl.pallas_call(
        functools.partial(_qkv_sample_kernel, q_scale=HEAD_DIM ** -0.5),
        grid=(n // tm,),
        in_specs=[pl.BlockSpec((tm, d), lambda i: (i, 0)), mod, mod,
                  _const_spec((d, da)), _const_spec((d, da)), _const_spec((d, da))],
        out_specs=[row, row, row],
        out_shape=[jax.ShapeDtypeStruct((n, da), BF16),
                   jax.ShapeDtypeStruct((n, da), F32),
                   jax.ShapeDtypeStruct((n, da), F32)],
        compiler_params=_params("arbitrary"),
        name="qkv_sample",
    )(x, shift, scale, wq, wk, wv)


def _attn_prompt_kernel(bias_ref, q_ref, kt_ref, v_ref, u_ref, o_ref, *, tq, tk):
    g = pl.program_id(1)
    qi = pl.program_id(2)
    q2 = q_ref[...]
    lane = lax.broadcasted_iota(jnp.int32, (tq, LANES), 1)
    qpos = qi * tq + lax.broadcasted_iota(jnp.int32, (tq, tk), 0)
    kcol = lax.broadcasted_iota(jnp.int32, (tq, tk), 1)
    n_sub = (qi + 1) * (tq // tk)
    u = u_ref[...]
    outs = []
    for p in range(HEADS_PER_GROUP):
        qp = jnp.where(lane // HEAD_DIM == p, q2, jnp.zeros_like(q2))
        bias = bias_ref[g * HEADS_PER_GROUP + p]

        def body(j, carry, qp=qp, bias=bias):
            o, c = carry
            off = pl.multiple_of((n_sub - 1 - j) * tk, tk)
            z = jnp.dot(qp, kt_ref[0, :, pl.ds(off, tk)], preferred_element_type=F32) + bias
            vis = (off + kcol) < qpos
            lm = jnp.where(vis, _softplus(z), 0.0)
            incl = jnp.dot(lm.astype(BF16), u, preferred_element_type=F32)
            a = jnp.where(vis, jnp.exp(z - incl - c), 0.0)
            o = o + jnp.dot(a.astype(BF16), v_ref[pl.ds(off, tk), :], preferred_element_type=F32)
            return o, c + incl[:, 0:1]

        o, _ = lax.fori_loop(0, n_sub, body, (jnp.zeros((tq, LANES), F32), jnp.zeros((tq, 1), F32)))
        outs.append(o)
    o_ref[...] = jnp.where(lane < HEAD_DIM, outs[0], outs[1]).astype(o_ref.dtype)


def _suffix_ones(n):
    j = lax.broadcasted_iota(jnp.int32, (n, n), 0)
    s = lax.broadcasted_iota(jnp.int32, (n, n), 1)
    return (j >= s).astype(BF16)


def _attn_prompt(q, ktb, vb, sb_bias, n_seq, tq, tk):
    n, da = q.shape
    s_len = n // n_seq
    nq = s_len // tq
    n_groups = da // LANES
    return pl.pallas_call(
        functools.partial(_attn_prompt_kernel, tq=tq, tk=tk),
        grid_spec=pltpu.PrefetchScalarGridSpec(
            num_scalar_prefetch=1,
            grid=(n_seq, n_groups, nq),
            in_specs=[pl.BlockSpec((tq, LANES), lambda b, g, i, *_: (b * nq + i, g)),
                      pl.BlockSpec((1, LANES, s_len), lambda b, g, i, *_: (b, g, 0)),
                      pl.BlockSpec((s_len, LANES), lambda b, g, i, *_: (b, g)),
                      pl.BlockSpec((tk, tk), lambda b, g, i, *_: (0, 0))],
            out_specs=pl.BlockSpec((tq, LANES), lambda b, g, i, *_: (b * nq + i, g)),
        ),
        out_shape=jax.ShapeDtypeStruct((n, da), BF16),
        compiler_params=_params("arbitrary", "arbitrary", "arbitrary"),
        name="attn_prompt",
    )(sb_bias, q, ktb, vb, _suffix_ones(tk))


def _attn_sample_kernel(pt_ref, qbd_ref, bias_ref, knew_ref, vnew_ref, u_ref, *rest,
                        pages_per_step, n_heads, t_new):
    k_refs = rest[:pages_per_step]
    v_refs = rest[pages_per_step:2 * pages_per_step]
    o_ref, acc_ref, carry_ref, kbuf, vbuf = rest[2 * pages_per_step:]
    j = pl.program_id(1)
    n_rows = t_new * n_heads
    page = kbuf.shape[1]
    qbd = qbd_ref[0]
    bias = bias_ref[...]
    u = u_ref[...]

    def block(kt, vt, vis, o, c):
        z = jnp.dot(qbd, kt.astype(BF16), preferred_element_type=F32) + bias
        sp = _softplus(z)
        lm = sp if vis is None else jnp.where(vis, sp, 0.0)
        incl = jnp.dot(lm.astype(BF16), u, preferred_element_type=F32)
        e = jnp.exp(z - incl - c)
        a = e if vis is None else jnp.where(vis, e, 0.0)
        o = o + lax.dot_general(a.astype(BF16), vt.astype(BF16), _NT, preferred_element_type=F32)
        return o, c + incl[:, 0:1]

    @pl.when(j == 0)
    def _():
        kbuf[...] = jnp.zeros(kbuf.shape, F32)
        vbuf[...] = jnp.zeros(vbuf.shape, F32)
        kbuf[:, 0:t_new] = knew_ref[0]
        vbuf[:, 0:t_new] = vnew_ref[0]
        row_t = lax.broadcasted_iota(jnp.int32, (n_rows, page), 0) // n_heads
        col = lax.broadcasted_iota(jnp.int32, (n_rows, page), 1)
        o, c = block(kbuf[...], vbuf[...], col < row_t,
                     jnp.zeros(acc_ref.shape, F32), jnp.zeros(carry_ref.shape, F32))
        acc_ref[...] = o
        carry_ref[...] = c

    o = acc_ref[...]
    c = carry_ref[...]
    for i in range(pages_per_step):
        o, c = block(k_refs[i][0], v_refs[i][0], None, o, c)
    acc_ref[...] = o
    carry_ref[...] = c

    @pl.when(j == pl.num_programs(1) - 1)
    def _():
        r = lax.broadcasted_iota(jnp.int32, o.shape, 0)
        l = lax.broadcasted_iota(jnp.int32, o.shape, 1)
        own = jnp.where(l // HEAD_DIM == r % n_heads, o, 0.0)
        o_ref[0] = own.reshape(t_new, n_heads, o.shape[1]).sum(axis=1)


def _attn_sample(q, k_new, v_new, sb_bias, cache_kt, cache_vt, page_table, t_new, pages_per_step):
    n, da = q.shape
    n_seq = n // t_new
    n_heads = da // HEAD_DIM
    n_pages = page_table.shape[1]
    page = cache_kt.shape[2]
    n_rows = t_new * n_heads
    assert n_pages % pages_per_step == 0 and t_new <= page
    head_of_col = jnp.arange(da, dtype=jnp.int32) // HEAD_DIM
    blockdiag = head_of_col[None, :] == jnp.arange(n_heads, dtype=jnp.int32)[:, None]
    qbd = jnp.where(blockdiag[None, None], q.reshape(n_seq, t_new, 1, da), jnp.zeros((), q.dtype))
    qbd = qbd.reshape(n_seq, n_rows, da)
    bias_rows = jnp.tile(sb_bias.astype(F32), t_new).reshape(n_rows, 1)
    knew_t = k_new.reshape(n_seq, t_new, da).transpose(0, 2, 1)
    vnew_t = v_new.reshape(n_seq, t_new, da).transpose(0, 2, 1)

    def page_spec(i):
        def index_map(b, j, pt):
            return (pt[b * n_pages + (n_pages - 1 - (j * pages_per_step + i))], 0, 0)
        return pl.BlockSpec((1, da, page), index_map)

    seq3 = lambda b, j, pt: (b, 0, 0)
    const2 = lambda b, j, pt: (0, 0)
    specs = [pl.BlockSpec((1, n_rows, da), seq3),
             pl.BlockSpec((n_rows, 1), const2),
             pl.BlockSpec((1, da, t_new), seq3),
             pl.BlockSpec((1, da, t_new), seq3),
             pl.BlockSpec((page, page), const2)]
    specs += [page_spec(i) for i in range(pages_per_step)] * 2
    out = pl.pallas_call(
        functools.partial(_attn_sample_kernel, pages_per_step=pages_per_step, n_heads=n_heads, t_new=t_new),
        grid_spec=pltpu.PrefetchScalarGridSpec(
            num_scalar_prefetch=1,
            grid=(n_seq, n_pages // pages_per_step),
            in_specs=specs,
            out_specs=pl.BlockSpec((1, t_new, da), seq3),
            scratch_shapes=[pltpu.VMEM((n_rows, da), F32), pltpu.VMEM((n_rows, 1), F32),
                            pltpu.VMEM((da, page), F32), pltpu.VMEM((da, page), F32)],
        ),
        out_shape=jax.ShapeDtypeStruct((n_seq, t_new, da), F32),
        compiler_params=_params("arbitrary", "arbitrary"),
        name="attn_sample",
    )(page_table.reshape(-1), qbd, bias_rows, knew_t, vnew_t, _suffix_ones(page),
      *([cache_kt] * pages_per_step), *([cache_vt] * pages_per_step))
    return out.reshape(n, da)


def _post_kernel(i, *refs, prompt, tiles_per_seq, t_new, alpha, d_conv, n_experts):
    if prompt:
        (x_ref, attn_ref, sh1, sc1, g1, sh2, sc2, wr_ref, wc_ref, wpa_ref, wpb_ref, wo_ref,
         ln1g_ref, ln1b_ref, wrt_ref, br_ref,
         x1_ref, h2_ref, idx_ref, p_ref, cu_ref, tail_ref) = refs
    else:
        (x_ref, attn_ref, sh1, sc1, g1, sh2, sc2, p1_ref, p2_ref, wr_ref, wc_ref, wpa_ref, wpb_ref, wo_ref,
         ln1g_ref, ln1b_ref, wrt_ref, br_ref,
         x1_ref, h2_ref, idx_ref, p_ref, cu_ref) = refs
    x = x_ref[...]
    tm, d = x.shape
    dc = d_conv
    h = (_layer_norm(x) * (1.0 + sc1[0]) + sh1[0]).astype(BF16)
    y = jnp.dot(h, wr_ref[...], preferred_element_type=F32)
    gate_b = y[:, dc:2 * dc]
    cu = y[:, 2 * dc:3 * dc] * y[:, 0:dc]
    r1 = pltpu.roll(cu, 1, 0)
    r2 = pltpu.roll(cu, 2, 0)
    if prompt:
        first = (i % tiles_per_seq) == 0
        tail = jnp.where(first, 0.0, tail_ref[...])
        row8 = lax.broadcasted_iota(jnp.int32, (SUBLANES, dc), 0)
        f1 = jnp.where(row8 >= 1, r1[0:SUBLANES], pltpu.roll(tail, 1, 0))
        f2 = jnp.where(row8 >= 2, r2[0:SUBLANES], pltpu.roll(tail, 2, 0))
        z1 = jnp.concatenate([f1, r1[SUBLANES:]], axis=0)
        z2 = jnp.concatenate([f2, r2[SUBLANES:]], axis=0)
        tail_ref[...] = cu[tm - SUBLANES:, :]
        cu_ref[0] = cu[tm - SUBLANES:, :]
    else:
        tpos = lax.broadcasted_iota(jnp.int32, (tm, dc), 0) % t_new
        z1 = jnp.where(tpos >= 1, r1, p1_ref[...])
        z2 = jnp.where(tpos >= 2, r2, p2_ref[...])
        cu_ref[...] = cu
    wc = wc_ref[...]
    conv = wc[0:1] * z2 + wc[1:2] * z1 + wc[2:3] * cu
    branch_b = jnp.dot((gate_b * conv).astype(BF16), wpb_ref[...], preferred_element_type=F32)
    branch_a = jnp.dot(attn_ref[...], wpa_ref[...], preferred_element_type=F32)
    merged = (jax.nn.sigmoid(y[:, 3 * dc:3 * dc + d]) * branch_a
              + jax.nn.sigmoid(y[:, 3 * dc + d:]) * branch_b)
    yo = jnp.dot(merged.astype(BF16), wo_ref[...], preferred_element_type=F32)
    x1 = _layer_norm(alpha * x + (1.0 + g1[0]) * yo) * ln1g_ref[...] + ln1b_ref[...]
    x1_ref[...] = x1
    h2 = _layer_norm(x1) * (1.0 + sc2[0]) + sh2[0]
    h2_ref[...] = h2
    logits = lax.dot_general(h2, wrt_ref[...], _NT, preferred_element_type=F32,
                             precision=lax.Precision.HIGHEST) + br_ref[...]
    lane_e = lax.broadcasted_iota(jnp.int32, logits.shape, 1)
    lane_o = lax.broadcasted_iota(jnp.int32, (tm, LANES), 1)
    idx_out = jnp.zeros((tm, LANES), jnp.int32)
    e_out = jnp.zeros((tm, LANES), F32)
    denom = jnp.zeros((tm, 1), F32)
    top = None
    for k in range(TOP_K):
        m = jnp.max(logits, axis=-1, keepdims=True)
        sel = jnp.min(jnp.where(logits == m, lane_e, n_experts), axis=-1, keepdims=True)
        logits = jnp.where(lane_e == sel, -jnp.inf, logits)
        if top is None:
            top = m
        e = jnp.exp(m - top)
        denom = denom + e
        idx_out = jnp.where(lane_o == k, sel, idx_out)
        e_out = jnp.where(lane_o == k, e, e_out)
    idx_ref[...] = idx_out
    p_ref[...] = e_out / denom


def _post(x, attn, mods, conv_hist, w, *, prompt, n_seq, t_new, tm, alpha, h2_rows, h2_all, row_offset):
    n, d = x.shape
    da = attn.shape[1]
    dc = w["w_conv"].shape[1]
    n_exp = w["w_router_t"].shape[0]
    n_tiles = n // tm
    assert row_offset % tm == 0 and (h2_rows - n) % tm == 0
    off_blocks = row_offset // tm
    n_fill = (h2_rows - n) // tm if h2_all is None else 0
    assert n_fill == 0 or row_offset == 0
    tiles_per_seq = (n // n_seq) // tm if prompt else 0
    tile = lambda i: jnp.minimum(i, n_tiles - 1)
    if prompt:
        mod = pl.BlockSpec((1, 1, d), lambda i: (tile(i) // tiles_per_seq, 0, 0))
    else:
        mod = pl.BlockSpec((1, tm, d), lambda i: (tile(i), 0, 0))
    rows = lambda c: pl.BlockSpec((tm, c), lambda i: (tile(i), 0))
    in_specs = [rows(d), rows(da)] + [mod] * 5
    args = [x, attn, *mods]
    if not prompt:
        in_specs += [rows(dc), rows(dc)]
        args += list(conv_hist)
    weights = [w["w_rest"], w["w_conv"], w["w_pa"], w["w_pb"], w["w_o"], w["ln1_g"], w["ln1_b"],
               w["w_router_t"], w["b_router"]]
    in_specs += [_const_spec(a.shape) for a in weights]
    args += weights
    out_specs = [rows(d),
                 pl.BlockSpec((tm, d), lambda i: (i + off_blocks, 0)),
                 rows(LANES), rows(LANES)]
    out_shape = [jax.ShapeDtypeStruct((n, d), F32),
                 jax.ShapeDtypeStruct((h2_rows, d), F32),
                 jax.ShapeDtypeStruct((n, LANES), jnp.int32),
                 jax.ShapeDtypeStruct((n, LANES), F32)]
    scratch = []
    if prompt:
        out_specs.append(pl.BlockSpec((1, SUBLANES, dc), lambda i: (tile(i) // tiles_per_seq, 0, 0)))
        out_shape.append(jax.ShapeDtypeStruct((n_seq, SUBLANES, dc), F32))
        scratch.append(pltpu.VMEM((SUBLANES, dc), F32))
    else:
        out_specs.append(rows(dc))
        out_shape.append(jax.ShapeDtypeStruct((n, dc), F32))
    aliases = {}
    if h2_all is not None:
        in_specs.append(pl.BlockSpec(memory_space=pl.ANY))
        args.append(h2_all)
        aliases = {len(args) - 1: 1}
    n_in = len(args)
    h2_pos = n_in + 1 - len(aliases)

    def body(*refs):
        if h2_all is not None:
            refs = refs[:n_in - 1] + refs[n_in:]
        i = pl.program_id(0)

        @pl.when(i < n_tiles)
        def _():
            _post_kernel(i, *refs, prompt=prompt, tiles_per_seq=tiles_per_seq, t_new=t_new, alpha=alpha,
                         d_conv=dc, n_experts=n_exp)

        if n_fill:
            @pl.when(i >= n_tiles)
            def _():
                refs[h2_pos][...] = jnp.zeros((tm, d), F32)

    return pl.pallas_call(
        body,
        grid=(n_tiles + n_fill,),
        in_specs=in_specs,
        out_specs=out_specs,
        out_shape=out_shape,
        scratch_shapes=scratch,
        input_output_aliases=aliases,
        compiler_params=_params("arbitrary"),
        name="post_prompt" if prompt else "post_sample",
    )(*args)


def _route(top_idx, n_experts):
    n_tok = top_idx.shape[0]
    n_assign = n_tok * TOP_K
    flat_e = top_idx.reshape(-1)
    order = jnp.argsort(flat_e).astype(jnp.int32)
    se = flat_e[order]
    counts = jnp.bincount(flat_e, length=n_experts).astype(jnp.int32)
    padded = (counts + MOE_BLOCK - 1) // MOE_BLOCK * MOE_BLOCK
    pad_end = jnp.cumsum(padded)
    pad_start = pad_end - padded
    start = jnp.cumsum(counts) - counts
    dest = pad_start[se] + jnp.arange(n_assign, dtype=jnp.int32) - start[se]
    n_blocks = (n_assign + MOE_BLOCK - 1) // MOE_BLOCK + n_experts
    n_slots = n_blocks * MOE_BLOCK
    tok = order // TOP_K
    which = order % TOP_K
    src = jnp.zeros((n_slots,), jnp.int32).at[dest].set(tok)
    dst = jnp.zeros((n_slots,), jnp.int32).at[dest].set(which * n_tok + tok)
    block_lo = jnp.arange(n_blocks, dtype=jnp.int32) * MOE_BLOCK
    block_e = jnp.minimum(jnp.searchsorted(pad_end, block_lo, side="right"), n_experts - 1).astype(jnp.int32)
    n_valid = jnp.clip(pad_start[block_e] + counts[block_e] - block_lo, 0, MOE_BLOCK).astype(jnp.int32)
    return src.reshape(n_blocks, 1, MOE_BLOCK), dst.reshape(n_blocks, 1, MOE_BLOCK), block_e, n_valid


def _moe_kernel(be_ref, nv_ref, src_ref, dst_ref, h2_hbm, wg_ref, bg_ref, wu_ref, bu_ref, wd_ref, bd_ref,
                out_hbm, xbuf, ybuf, gsem, ssem):
    i = pl.program_id(0)
    n_valid = nv_ref[i]

    def gather(r):
        return pltpu.make_async_copy(h2_hbm.at[pl.ds(src_ref[0, 0, r], 1)], xbuf.at[pl.ds(r, 1)], gsem)

    def scatter(r):
        return pltpu.make_async_copy(ybuf.at[pl.ds(r, 1)], out_hbm.at[pl.ds(dst_ref[0, 0, r], 1)], ssem)

    def for_valid_rows(fn):
        def step(r, carry):
            fn(r)
            return carry
        lax.fori_loop(0, n_valid, step, 0)

    @pl.when(i == 0)
    def _():
        xbuf[...] = jnp.zeros(xbuf.shape, F32)

    @pl.when(n_valid > 0)
    def _():
        for_valid_rows(lambda r: gather(r).start())
        for_valid_rows(lambda r: gather(r).wait())
        x = xbuf[...].astype(BF16)
        g = jnp.minimum(jnp.dot(x, wg_ref[0], preferred_element_type=F32) + bg_ref[0], SWIGLU_LIMIT)
        u = jnp.clip(jnp.dot(x, wu_ref[0], preferred_element_type=F32) + bu_ref[0], -SWIGLU_LIMIT, SWIGLU_LIMIT)
        act = g * jax.nn.sigmoid(SWIGLU_ALPHA * g) * (u + 1.0)
        ybuf[...] = jnp.dot(act.astype(BF16), wd_ref[0], preferred_element_type=F32) + bd_ref[0]
        for_valid_rows(lambda r: scatter(r).start())
        for_valid_rows(lambda r: scatter(r).wait())


def _moe(h2_all, src, dst, block_e, n_valid, wg, bg, wu, bu, wd, bd):
    n_tok, d = h2_all.shape
    n_blocks = src.shape[0]
    n_exp, _, de = wg.shape
    slot = pl.BlockSpec((1, 1, MOE_BLOCK), lambda i, be, nv: (i, 0, 0), memory_space=pltpu.SMEM)
    wspec = lambda a, b: pl.BlockSpec((1, a, b), lambda i, be, nv: (be[i], 0, 0))
    return pl.pallas_call(
        _moe_kernel,
        grid_spec=pltpu.PrefetchScalarGridSpec(
            num_scalar_prefetch=2,
            grid=(n_blocks,),
            in_specs=[slot, slot, pl.BlockSpec(memory_space=pl.ANY),
                      wspec(d, de), wspec(1, de), wspec(d, de), wspec(1, de), wspec(de, d), wspec(1, d)],
            out_specs=pl.BlockSpec(memory_space=pl.ANY),
            scratch_shapes=[pltpu.VMEM((MOE_BLOCK, d), F32), pltpu.VMEM((MOE_BLOCK, d), F32),
                            pltpu.SemaphoreType.DMA(()), pltpu.SemaphoreType.DMA(())],
        ),
        out_shape=jax.ShapeDtypeStruct((TOP_K * n_tok, d), F32),
        compiler_params=_params("arbitrary"),
        name="moe",
    )(block_e, n_valid, src, dst, h2_all, wg, bg.reshape(n_exp, 1, de), wu, bu.reshape(n_exp, 1, de),
      wd, bd.reshape(n_exp, 1, d))


def _final_kernel(x1_ref, p_ref, g2_ref, y0, y1, y2, y3, lng_ref, lnb_ref, o_ref, *, alpha):
    p = p_ref[...]
    y = (p[:, 0:1] * y0[...] + p[:, 1:2] * y1[...]) + (p[:, 2:3] * y2[...] + p[:, 3:4] * y3[...])
    o_ref[...] = _layer_norm(alpha * x1_ref[...] + (1.0 + g2_ref[0]) * y) * lng_ref[...] + lnb_ref[...]


def _final(x1, probs, gate2, y_all, ln_g, ln_b, *, prompt, n_seq, tm, alpha, row_offset, n_tok_all):
    n, d = x1.shape
    if prompt:
        tiles_per_seq = (n // n_seq) // tm
        mod = pl.BlockSpec((1, 1, d), lambda i: (i // tiles_per_seq, 0, 0))
    else:
        mod = pl.BlockSpec((1, tm, d), lambda i: (i, 0, 0))
    rows = lambda c: pl.BlockSpec((tm, c), lambda i: (i, 0))

    assert n_tok_all % tm == 0 and row_offset % tm == 0

    def yspec(k):
        base = (k * n_tok_all + row_offset) // tm
        return pl.BlockSpec((tm, d), lambda i: (base + i, 0))

    return pl.pallas_call(
        functools.partial(_final_kernel, alpha=alpha),
        grid=(n // tm,),
        in_specs=[rows(d), rows(LANES), mod] + [yspec(k) for k in range(TOP_K)]
                 + [_const_spec((1, d)), _const_spec((1, d))],
        out_specs=rows(d),
        out_shape=jax.ShapeDtypeStruct((n, d), F32),
        compiler_params=_params("arbitrary"),
        name="final_prompt" if prompt else "final_sample",
    )(x1, probs, gate2, *([y_all] * TOP_K), ln_g, ln_b)


def _pick(n, candidates):
    for c in candidates:
        if n % c == 0:
            return c
    raise ValueError(f"no tile size in {candidates} divides {n}")


def _decoder_layer(xp, xs, c_all, cache_kt, cache_vt, conv_state, page_table, lw, *, n_p, n_s, t_new, depth):
    d = xp.shape[1]
    s_len = xp.shape[0] // n_p
    n_rows_s = xs.shape[0]
    alpha = (2 * depth) ** 0.25
    d_attn = lw["w_pa"].shape[0]
    d_conv = lw["w_conv"].shape[1]
    n_exp = lw["w_router"].shape[1]

    ada = _ada(c_all, lw["w_ada"], lw["b_ada"])
    mods = [ada[:, k * d:(k + 1) * d] for k in range(6)]
    tm_s = _pick(n_rows_s, (256, 128, 64, 32, 16, 8))
    mods_p = [m[:n_p].reshape(n_p, 1, d) for m in mods]
    mods_s = [jnp.repeat(m[n_p:], t_new, axis=0).reshape(n_rows_s // tm_s, tm_s, d) for m in mods]

    w_in = lw["w_in"]
    wq = w_in[:, :d_attn].astype(BF16)
    wk = w_in[:, d_attn:2 * d_attn].astype(BF16)
    wv = w_in[:, 2 * d_attn:3 * d_attn].astype(BF16)
    weights = {
        "w_rest": w_in[:, 3 * d_attn:].astype(BF16),
        "w_conv": lw["w_conv"],
        "w_pa": lw["w_pa"].astype(BF16), "w_pb": lw["w_pb"].astype(BF16), "w_o": lw["w_o"].astype(BF16),
        "ln1_g": lw["ln1_g"].reshape(1, d), "ln1_b": lw["ln1_b"].reshape(1, d),
        "w_router_t": lw["w_router"].T, "b_router": lw["b_router"].reshape(1, n_exp),
    }

    tm_p = _pick(s_len, (512, 256, 128))
    q_p, kt_p, ktb_p, vt_p, vb_p = _qkv_prompt(xp, mods_p[0], mods_p[1], wq, wk.T, wv.T, wv, n_p, tm_p)
    tq = _pick(s_len, (512, 256))
    attn_p = _attn_prompt(q_p, ktb_p, vb_p, lw["sb_bias"], n_p, tq, 256)

    q_s, k_s, v_s = _qkv_sample(xs, mods_s[0], mods_s[1], wq, wk, wv, tm_s)
    pages_per_step = _pick(page_table.shape[1], (8, 4, 2, 1))
    attn_s = _attn_sample(q_s, k_s, v_s, lw["sb_bias"], cache_kt, cache_vt, page_table, t_new,
                          pages_per_step).astype(BF16)

    n_tok_p = xp.shape[0]
    n_tok_all = n_tok_p + n_rows_s
    tm_post = _pick(s_len, (256, 128))
    x1_p, h2_all, idx_p, pr_p, tail_p = _post(
        xp, attn_p, mods_p[:5], None, weights, prompt=True, n_seq=n_p, t_new=t_new, tm=tm_post,
        alpha=alpha, h2_rows=n_tok_all, h2_all=None, row_offset=0)
    tpos = jnp.arange(n_rows_s, dtype=jnp.int32) % t_new
    prev = conv_state
    prev_rep = jnp.repeat(prev, t_new, axis=0)
    p1 = jnp.where((tpos == 0)[:, None], prev_rep[:, 1], 0.0)
    p2 = jnp.where((tpos == 0)[:, None], prev_rep[:, 0], jnp.where((tpos == 1)[:, None], prev_rep[:, 1], 0.0))
    x1_s, h2_all, idx_s, pr_s, cu_s = _post(
        xs, attn_s, mods_s[:5], (p1, p2), weights, prompt=False, n_seq=n_s, t_new=t_new, tm=tm_s,
        alpha=alpha, h2_rows=n_tok_all, h2_all=h2_all, row_offset=n_tok_p)

    top_idx = jnp.concatenate([idx_p[:, :TOP_K], idx_s[:, :TOP_K]], axis=0)
    src, dst, block_e, n_used = _route(top_idx, n_exp)
    y_all = _moe(h2_all, src, dst, block_e, n_used,
                 lw["w_gate"].astype(BF16), lw["b_gate"], lw["w_up"].astype(BF16), lw["b_up"],
                 lw["w_down"].astype(BF16), lw["b_down"])

    ln2_g = lw["ln2_g"].reshape(1, d)
    ln2_b = lw["ln2_b"].reshape(1, d)
    out_p = _final(x1_p, pr_p, mods_p[5], y_all, ln2_g, ln2_b, prompt=True, n_seq=n_p, tm=tm_post,
                   alpha=alpha, row_offset=0, n_tok_all=n_tok_all)
    out_s = _final(x1_s, pr_s, mods_s[5], y_all, ln2_g, ln2_b, prompt=False, n_seq=n_s, tm=tm_s,
                   alpha=alpha, row_offset=n_tok_p, n_tok_all=n_tok_all)

    n_heads = d_attn // HEAD_DIM
    k_prompt = kt_p.reshape(n_p, n_heads, HEAD_DIM, s_len).transpose(0, 3, 1, 2)
    v_prompt = vt_p.reshape(n_p, n_heads, HEAD_DIM, s_len).transpose(0, 3, 1, 2)
    conv_prompt = tail_p[:, SUBLANES - 2:, :]
    k_sample = k_s.reshape(n_s, t_new, n_heads, HEAD_DIM)
    v_sample = v_s.reshape(n_s, t_new, n_heads, HEAD_DIM)
    conv_sample = cu_s.reshape(n_s, t_new, d_conv)[:, t_new - 2:, :]
    return out_p, out_s, k_prompt, v_prompt, conv_prompt, k_sample, v_sample, conv_sample


def kernel(x_prompt, x_sample, cache_k, cache_v, state_conv, page_table, c_prompt, c_sample, w_ada, b_ada, w_in, sb_bias, w_conv, w_pa, w_pb, w_o, ln1_g, ln1_b, w_router, b_router, w_gate, b_gate, w_up, b_up, w_down, b_down, ln2_g, ln2_b):
    depth = w_in.shape[0]
    n_p, s_len, d = x_prompt.shape
    n_s, t_new, _ = x_sample.shape
    assert t_new >= 2 and state_conv.shape[2] == 2
    n_pool, page, n_heads, head_dim = cache_k.shape[1:]
    assert head_dim == HEAD_DIM
    names = ("w_ada", "b_ada", "w_in", "sb_bias", "w_conv", "w_pa", "w_pb", "w_o", "ln1_g", "ln1_b",
             "w_router", "b_router", "w_gate", "b_gate", "w_up", "b_up", "w_down", "b_down", "ln2_g", "ln2_b")
    stacked = (w_ada, b_ada, w_in, sb_bias, w_conv, w_pa, w_pb, w_o, ln1_g, ln1_b,
               w_router, b_router, w_gate, b_gate, w_up, b_up, w_down, b_down, ln2_g, ln2_b)
    xp = x_prompt.reshape(n_p * s_len, d)
    xs = x_sample.reshape(n_s * t_new, d)
    c_all = jnp.concatenate([c_prompt, c_sample], axis=0)
    outs = [[] for _ in range(6)]
    for l in range(depth):
        lw = {n: a[l] for n, a in zip(names, stacked)}
        cache_kt = cache_k[l].transpose(0, 2, 3, 1).reshape(n_pool, n_heads * head_dim, page)
        cache_vt = cache_v[l].transpose(0, 2, 3, 1).reshape(n_pool, n_heads * head_dim, page)
        xp, xs, *rest = _decoder_layer(xp, xs, c_all, cache_kt, cache_vt, state_conv[l], page_table, lw,
                                       n_p=n_p, n_s=n_s, t_new=t_new, depth=depth)
        for o, r in zip(outs, rest):
            o.append(r)
    stack = [jnp.stack(o) for o in outs]
    return (xp.reshape(n_p, s_len, d), xs.reshape(n_s, t_new, d), *stack)
```

```python
import functools

import jax
import jax.numpy as jnp
from jax import lax
from jax.experimental import pallas as pl
from jax.experimental.pallas import tpu as pltpu

F32 = jnp.float32
BF16 = jnp.bfloat16

HEAD_DIM = 64
TOP_K = 4
MOE_BLOCK = 256
SWIGLU_LIMIT = 7.0
SWIGLU_ALPHA = 1.702
LN_EPS = 1e-5

LANES = 128
SUBLANES = 8
HEADS_PER_GROUP = LANES // HEAD_DIM
VMEM_LIMIT = 56 * 1024 * 1024

_NT = (((1,), (1,)), ((), ()))


def _params(*sem):
    return pltpu.CompilerParams(dimension_semantics=sem, vmem_limit_bytes=VMEM_LIMIT)


def _layer_norm(x):
    mu = jnp.mean(x, axis=-1, keepdims=True)
    xc = x - mu
    var = jnp.mean(xc * xc, axis=-1, keepdims=True)
    return xc * lax.rsqrt(var + LN_EPS)


def _softplus(z):
    neg_abs = lax.bitcast_convert_type(lax.bitcast_convert_type(z, jnp.int32) | jnp.int32(-2 ** 31), F32)
    return jnp.maximum(z, 0.0) + jnp.log(1.0 + jnp.exp(neg_abs))


def _const_spec(shape):
    return pl.BlockSpec(shape, lambda *_: (0,) * len(shape))


def _ada_kernel(c_ref, w_ref, b_ref, o_ref):
    o_ref[...] = jnp.dot(c_ref[...], w_ref[...], preferred_element_type=F32,
                         precision=lax.Precision.HIGHEST) + b_ref[...]


def _ada(c, w, b):
    m, d = c.shape
    n = w.shape[1]
    tn = n // 6
    return pl.pallas_call(
        _ada_kernel,
        grid=(n // tn,),
        in_specs=[_const_spec((m, d)),
                  pl.BlockSpec((d, tn), lambda j: (0, j)),
                  pl.BlockSpec((1, tn), lambda j: (0, j))],
        out_specs=pl.BlockSpec((m, tn), lambda j: (0, j)),
        out_shape=jax.ShapeDtypeStruct((m, n), F32),
        compiler_params=_params("arbitrary"),
        name="ada",
    )(c, w, b.reshape(1, n))


def _qkv_prompt_kernel(x_ref, shift_ref, scale_ref, wq_ref, wkt_ref, wvt_ref, wv_ref,
                       q_ref, kt_ref, ktb_ref, vt_ref, vb_ref, *, q_scale):
    h = (_layer_norm(x_ref[...]) * (1.0 + scale_ref[0]) + shift_ref[0]).astype(BF16)
    q = jnp.dot(h, wq_ref[...], preferred_element_type=F32)
    q_ref[...] = (q * q_scale).astype(BF16)
    kt = lax.dot_general(wkt_ref[...], h, _NT, preferred_element_type=F32)
    kt_ref[0] = kt
    ktb_ref[0] = kt.astype(BF16)
    vt_ref[0] = lax.dot_general(wvt_ref[...], h, _NT, preferred_element_type=F32)
    vb_ref[...] = jnp.dot(h, wv_ref[...], preferred_element_type=F32).astype(BF16)


def _qkv_prompt(x, shift, scale, wq, wkt, wvt, wv, n_seq, tm):
    n, d = x.shape
    da = wq.shape[1]
    s_len = n // n_seq
    tps = s_len // tm
    mod = pl.BlockSpec((1, 1, d), lambda i: (i // tps, 0, 0))
    row = pl.BlockSpec((tm, da), lambda i: (i, 0))
    tr = pl.BlockSpec((1, da, tm), lambda i: (i // tps, 0, i % tps))
    return pl.pallas_call(
        functools.partial(_qkv_prompt_kernel, q_scale=HEAD_DIM ** -0.5),
        grid=(n // tm,),
        in_specs=[pl.BlockSpec((tm, d), lambda i: (i, 0)), mod, mod,
                  _const_spec((d, da)), _const_spec((da, d)), _const_spec((da, d)), _const_spec((d, da))],
        out_specs=[row, tr, tr, tr, row],
        out_shape=[jax.ShapeDtypeStruct((n, da), BF16),
                   jax.ShapeDtypeStruct((n_seq, da, s_len), F32),
                   jax.ShapeDtypeStruct((n_seq, da, s_len), BF16),
                   jax.ShapeDtypeStruct((n_seq, da, s_len), F32),
                   jax.ShapeDtypeStruct((n, da), BF16)],
        compiler_params=_params("arbitrary"),
        name="qkv_prompt",
    )(x, shift, scale, wq, wkt, wvt, wv)


def _qkv_sample_kernel(x_ref, shift_ref, scale_ref, wq_ref, wk_ref, wv_ref, q_ref, k_ref, v_ref, *, q_scale):
    h = (_layer_norm(x_ref[...]) * (1.0 + scale_ref[0]) + shift_ref[0]).astype(BF16)
    q_ref[...] = (jnp.dot(h, wq_ref[...], preferred_element_type=F32) * q_scale).astype(BF16)
    k_ref[...] = jnp.dot(h, wk_ref[...], preferred_element_type=F32)
    v_ref[...] = jnp.dot(h, wv_ref[...], preferred_element_type=F32)


def _qkv_sample(x, shift, scale, wq, wk, wv, tm):
    n, d = x.shape
    da = wq.shape[1]
    mod = pl.BlockSpec((1, tm, d), lambda i: (i, 0, 0))
    row = pl.BlockSpec((tm, da), lambda i: (i, 0))
    return pl.pallas_call(
        functools.partial(_qkv_sample_kernel, q_scale=HEAD_DIM ** -0.5),
        grid=(n // tm,),
        in_specs=[pl.BlockSpec((tm, d), lambda i: (i, 0)), mod, mod,
                  _const_spec((d, da)), _const_spec((d, da)), _const_spec((d, da))],
        out_specs=[row, row, row],
        out_shape=[jax.ShapeDtypeStruct((n, da), BF16),
                   jax.ShapeDtypeStruct((n, da), F32),
                   jax.ShapeDtypeStruct((n, da), F32)],
        compiler_params=_params("arbitrary"),
        name="qkv_sample",
    )(x, shift, scale, wq, wk, wv)


def _attn_prompt_kernel(bias_ref, q_ref, kt_ref, v_ref, u_ref, o_ref, *, tq, tk):
    g = pl.program_id(1)
    qi = pl.program_id(2)
    q2 = q_ref[...]
    lane = lax.broadcasted_iota(jnp.int32, (tq, LANES), 1)
    n_diag = tq // tk
    n_off = qi * n_diag
    row = lax.broadcasted_iota(jnp.int32, (tq, tk), 0)
    col = lax.broadcasted_iota(jnp.int32, (tq, tk), 1)
    u = u_ref[...]
    heads = range(HEADS_PER_GROUP)
    qs = [jnp.where(lane // HEAD_DIM == p, q2, jnp.zeros_like(q2)) for p in heads]
    biases = [bias_ref[g * HEADS_PER_GROUP + p] for p in heads]

    def sub_block(off, state, vis):
        kt = kt_ref[0, :, pl.ds(off, tk)]
        v = v_ref[pl.ds(off, tk), :]
        new_state = []
        for qp, bias, (o, c) in zip(qs, biases, state):
            z = jnp.dot(qp, kt, preferred_element_type=F32) + bias
            sp = _softplus(z)
            if vis is not None:
                sp = jnp.where(vis, sp, 0.0)
            r = jnp.dot(sp.astype(BF16), u, preferred_element_type=F32)
            a = jnp.exp(z - r[:, :tk] - jnp.concatenate([c] * (tk // LANES), axis=1))
            if vis is not None:
                a = jnp.where(vis, a, 0.0)
            o = o + jnp.dot(a.astype(BF16), v, preferred_element_type=F32)
            new_state.append((o, c + r[:, tk:]))
        return tuple(new_state)

    def key_block(kb, state, masked):
        for d in reversed(range(n_diag)):
            off = pl.multiple_of(kb * tq + d * tk, tk)
            state = sub_block(off, state, (d * tk + col < row) if masked else None)
        return state

    zeros = jnp.zeros((tq, LANES), F32)
    state = key_block(qi, tuple((zeros, zeros) for _ in heads), True)
    state = lax.fori_loop(0, qi, lambda j, st: key_block(qi - 1 - j, st, False), state)
    o_ref[...] = jnp.where(lane < HEAD_DIM, state[0][0], state[1][0]).astype(o_ref.dtype)


def _suffix_ones(n, extra=0):
    j = lax.broadcasted_iota(jnp.int32, (n, n + extra), 0)
    s = lax.broadcasted_iota(jnp.int32, (n, n + extra), 1)
    return ((j >= s) | (s >= n)).astype(BF16)


def _attn_prompt(q, ktb, vb, sb_bias, n_seq, tq, tk):
    n, da = q.shape
    s_len = n // n_seq
    nq = s_len // tq
    n_groups = da // LANES
    return pl.pallas_call(
        functools.partial(_attn_prompt_kernel, tq=tq, tk=tk),
        grid_spec=pltpu.PrefetchScalarGridSpec(
            num_scalar_prefetch=1,
            grid=(n_seq, n_groups, nq),
            in_specs=[pl.BlockSpec((tq, LANES), lambda b, g, i, *_: (b * nq + i, g)),
                      pl.BlockSpec((1, LANES, s_len), lambda b, g, i, *_: (b, g, 0)),
                      pl.BlockSpec((s_len, LANES), lambda b, g, i, *_: (b, g)),
                      pl.BlockSpec((tk, tk + LANES), lambda b, g, i, *_: (0, 0))],
            out_specs=pl.BlockSpec((tq, LANES), lambda b, g, i, *_: (b * nq + i, g)),
        ),
        out_shape=jax.ShapeDtypeStruct((n, da), BF16),
        compiler_params=_params("arbitrary", "arbitrary", "arbitrary"),
        name="attn_prompt",
    )(sb_bias, q, ktb, vb, _suffix_ones(tk, LANES))


def _attn_sample_kernel(pt_ref, qbd_ref, bias_ref, knew_ref, vnew_ref, u_ref, *rest,
                        pages_per_step, n_heads, t_new):
    k_refs = rest[:pages_per_step]
    v_refs = rest[pages_per_step:2 * pages_per_step]
    o_ref, acc_ref, carry_ref, kbuf, vbuf = rest[2 * pages_per_step:]
    j = pl.program_id(1)
    n_rows = t_new * n_heads
    page = kbuf.shape[1]
    qbd = qbd_ref[0]
    bias = bias_ref[...]
    u = u_ref[...]

    def blocks(kts, vts, vis, o, c):
        zs = [jnp.dot(qbd, kt.astype(BF16), preferred_element_type=F32) + bias for kt in kts]
        sps = [_softplus(z) for z in zs]
        if vis is not None:
            sps = [jnp.where(vis, sp, 0.0) for sp in sps]
        rs = [jnp.dot(sp.astype(BF16), u, preferred_element_type=F32) for sp in sps]
        for z, r, vt in zip(zs, rs, vts):
            a = jnp.exp(z - r[:, :page] - jnp.concatenate([c] * (page // LANES), axis=1))
            if vis is not None:
                a = jnp.where(vis, a, 0.0)
            o = o + lax.dot_general(a.astype(BF16), vt.astype(BF16), _NT, preferred_element_type=F32)
            c = c + r[:, page:]
        return o, c

    @pl.when(j == 0)
    def _():
        kbuf[...] = jnp.zeros(kbuf.shape, F32)
        vbuf[...] = jnp.zeros(vbuf.shape, F32)
        kbuf[:, 0:t_new] = knew_ref[0]
        vbuf[:, 0:t_new] = vnew_ref[0]
        row_t = lax.broadcasted_iota(jnp.int32, (n_rows, page), 0) // n_heads
        col = lax.broadcasted_iota(jnp.int32, (n_rows, page), 1)
        o, c = blocks([kbuf[...]], [vbuf[...]], col < row_t,
                      jnp.zeros(acc_ref.shape, F32), jnp.zeros(carry_ref.shape, F32))
        acc_ref[...] = o
        carry_ref[...] = c

    o, c = blocks([r[0] for r in k_refs], [r[0] for r in v_refs], None, acc_ref[...], carry_ref[...])
    acc_ref[...] = o
    carry_ref[...] = c

    @pl.when(j == pl.num_programs(1) - 1)
    def _():
        r = lax.broadcasted_iota(jnp.int32, o.shape, 0)
        l = lax.broadcasted_iota(jnp.int32, o.shape, 1)
        own = jnp.where(l // HEAD_DIM == r % n_heads, o, 0.0)
        o_ref[0] = own.reshape(t_new, n_heads, o.shape[1]).sum(axis=1)


def _attn_sample(q, k_new, v_new, sb_bias, cache_kt, cache_vt, page_table, t_new, pages_per_step):
    n, da = q.shape
    n_seq = n // t_new
    n_heads = da // HEAD_DIM
    n_pages = page_table.shape[1]
    page = cache_kt.shape[2]
    n_rows = t_new * n_heads
    assert n_pages % pages_per_step == 0 and t_new <= page
    head_of_col = jnp.arange(da, dtype=jnp.int32) // HEAD_DIM
    blockdiag = head_of_col[None, :] == jnp.arange(n_heads, dtype=jnp.int32)[:, None]
    qbd = jnp.where(blockdiag[None, None], q.reshape(n_seq, t_new, 1, da), jnp.zeros((), q.dtype))
    qbd = qbd.reshape(n_seq, n_rows, da)
    assert page % LANES == 0
    bias_rows = jnp.broadcast_to(jnp.tile(sb_bias.astype(F32), t_new).reshape(n_rows, 1), (n_rows, page))
    knew_t = k_new.reshape(n_seq, t_new, da).transpose(0, 2, 1)
    vnew_t = v_new.reshape(n_seq, t_new, da).transpose(0, 2, 1)

    def page_spec(i):
        def index_map(b, j, pt):
            return (pt[b * n_pages + (n_pages - 1 - (j * pages_per_step + i))], 0, 0)
        return pl.BlockSpec((1, da, page), index_map)

    seq3 = lambda b, j, pt: (b, 0, 0)
    const2 = lambda b, j, pt: (0, 0)
    specs = [pl.BlockSpec((1, n_rows, da), seq3),
             pl.BlockSpec((n_rows, page), const2),
             pl.BlockSpec((1, da, t_new), seq3),
             pl.BlockSpec((1, da, t_new), seq3),
             pl.BlockSpec((page, page + LANES), const2)]
    specs += [page_spec(i) for i in range(pages_per_step)] * 2
    out = pl.pallas_call(
        functools.partial(_attn_sample_kernel, pages_per_step=pages_per_step, n_heads=n_heads, t_new=t_new),
        grid_spec=pltpu.PrefetchScalarGridSpec(
            num_scalar_prefetch=1,
            grid=(n_seq, n_pages // pages_per_step),
            in_specs=specs,
            out_specs=pl.BlockSpec((1, t_new, da), seq3),
            scratch_shapes=[pltpu.VMEM((n_rows, da), F32), pltpu.VMEM((n_rows, LANES), F32),
                            pltpu.VMEM((da, page), F32), pltpu.VMEM((da, page), F32)],
        ),
        out_shape=jax.ShapeDtypeStruct((n_seq, t_new, da), F32),
        compiler_params=_params("arbitrary", "arbitrary"),
        name="attn_sample",
    )(page_table.reshape(-1), qbd, bias_rows, knew_t, vnew_t, _suffix_ones(page, LANES),
      *([cache_kt] * pages_per_step), *([cache_vt] * pages_per_step))
    return out.reshape(n, da)


def _post_kernel(i, *refs, prompt, tiles_per_seq, t_new, alpha, d_conv, n_experts):
    if prompt:
        (x_ref, attn_ref, sh1, sc1, g1, sh2, sc2, cnt0_ref, lt_ref, wr_ref, wc_ref, wpa_ref, wpb_ref, wo_ref,
         ln1g_ref, ln1b_ref, wrt_ref, br_ref,
         x1_ref, h2_ref, idx_ref, p_ref, rank_ref, cnt_out_ref, cu_ref, cnt_ref, tail_ref) = refs
    else:
        (x_ref, attn_ref, sh1, sc1, g1, sh2, sc2, cnt0_ref, lt_ref, p1_ref, p2_ref,
         wr_ref, wc_ref, wpa_ref, wpb_ref, wo_ref, ln1g_ref, ln1b_ref, wrt_ref, br_ref,
         x1_ref, h2_ref, idx_ref, p_ref, rank_ref, cnt_out_ref, cu_ref, cnt_ref) = refs
    x = x_ref[...]
    tm, d = x.shape
    dc = d_conv
    h = (_layer_norm(x) * (1.0 + sc1[0]) + sh1[0]).astype(BF16)
    y = jnp.dot(h, wr_ref[...], preferred_element_type=F32)
    gate_b = y[:, dc:2 * dc]
    cu = y[:, 2 * dc:3 * dc] * y[:, 0:dc]
    r1 = pltpu.roll(cu, 1, 0)
    r2 = pltpu.roll(cu, 2, 0)
    if prompt:
        first = (i % tiles_per_seq) == 0
        tail = jnp.where(first, 0.0, tail_ref[...])
        row8 = lax.broadcasted_iota(jnp.int32, (SUBLANES, dc), 0)
        f1 = jnp.where(row8 >= 1, r1[0:SUBLANES], pltpu.roll(tail, 1, 0))
        f2 = jnp.where(row8 >= 2, r2[0:SUBLANES], pltpu.roll(tail, 2, 0))
        z1 = jnp.concatenate([f1, r1[SUBLANES:]], axis=0)
        z2 = jnp.concatenate([f2, r2[SUBLANES:]], axis=0)
        tail_ref[...] = cu[tm - SUBLANES:, :]
        cu_ref[0] = cu[tm - SUBLANES:, :]
    else:
        tpos = lax.broadcasted_iota(jnp.int32, (tm, dc), 0) % t_new
        z1 = jnp.where(tpos >= 1, r1, p1_ref[...])
        z2 = jnp.where(tpos >= 2, r2, p2_ref[...])
        cu_ref[...] = cu
    wc = wc_ref[...]
    conv = wc[0:1] * z2 + wc[1:2] * z1 + wc[2:3] * cu
    branch_b = jnp.dot((gate_b * conv).astype(BF16), wpb_ref[...], preferred_element_type=F32)
    branch_a = jnp.dot(attn_ref[...], wpa_ref[...], preferred_element_type=F32)
    merged = (jax.nn.sigmoid(y[:, 3 * dc:3 * dc + d]) * branch_a
              + jax.nn.sigmoid(y[:, 3 * dc + d:]) * branch_b)
    yo = jnp.dot(merged.astype(BF16), wo_ref[...], preferred_element_type=F32)
    x1 = _layer_norm(alpha * x + (1.0 + g1[0]) * yo) * ln1g_ref[...] + ln1b_ref[...]
    x1_ref[...] = x1
    h2 = _layer_norm(x1) * (1.0 + sc2[0]) + sh2[0]
    _rows_to_tiles(h2_ref, h2)
    logits = lax.dot_general(h2, wrt_ref[...], _NT, preferred_element_type=F32,
                             precision=lax.Precision.HIGHEST) + br_ref[...]
    lane_e = lax.broadcasted_iota(jnp.int32, logits.shape, 1)
    lane_o = lax.broadcasted_iota(jnp.int32, (tm, LANES), 1)
    idx_out = jnp.zeros((tm, LANES), jnp.int32)
    e_out = jnp.zeros((tm, LANES), F32)
    denom = jnp.zeros((tm, 1), F32)
    top = None
    sels = []
    for k in range(TOP_K):
        m = jnp.max(logits, axis=-1, keepdims=True)
        sel = jnp.min(jnp.where(logits == m, lane_e, n_experts), axis=-1, keepdims=True)
        logits = jnp.where(lane_e == sel, -jnp.inf, logits)
        if top is None:
            top = m
        e = jnp.exp(m - top)
        denom = denom + e
        idx_out = jnp.where(lane_o == k, sel, idx_out)
        e_out = jnp.where(lane_o == k, e, e_out)
        sels.append(sel)
    idx_ref[...] = idx_out
    p_ref[...] = e_out / denom

    @pl.when(i == 0)
    def _():
        cnt_ref[...] = cnt0_ref[...]

    base = cnt_ref[0:1, :]
    onehot = jnp.zeros((tm, LANES), F32)
    for sel in sels:
        onehot = onehot + (lane_o == sel).astype(F32)
    before = jnp.dot(lt_ref[...], onehot.astype(BF16), preferred_element_type=F32) + base
    rank_out = jnp.zeros((tm, LANES), jnp.int32)
    for k, sel in enumerate(sels):
        rk = jnp.sum(jnp.where(lane_o == sel, before, 0.0), axis=-1, keepdims=True)
        rank_out = jnp.where(lane_o == k, rk.astype(jnp.int32), rank_out)
    rank_ref[...] = rank_out
    cnt = jnp.broadcast_to(base + jnp.sum(onehot, axis=0, keepdims=True), cnt_ref.shape)
    cnt_ref[...] = cnt
    cnt_out_ref[...] = cnt


def _post(x, attn, mods, conv_hist, counts0, w, *, prompt, n_seq, t_new, tm, alpha):
    n, d = x.shape
    da = attn.shape[1]
    dc = w["w_conv"].shape[1]
    n_exp = w["w_router_t"].shape[0]
    assert n_exp <= LANES
    tiles_per_seq = (n // n_seq) // tm if prompt else 0
    if prompt:
        mod = pl.BlockSpec((1, 1, d), lambda i: (i // tiles_per_seq, 0, 0))
    else:
        mod = pl.BlockSpec((1, tm, d), lambda i: (i, 0, 0))
    rows = lambda c: pl.BlockSpec((tm, c), lambda i: (i, 0))
    t_row = lax.broadcasted_iota(jnp.int32, (tm, tm), 0)
    t_col = lax.broadcasted_iota(jnp.int32, (tm, tm), 1)
    strictly_lower = (t_col < t_row).astype(BF16)
    in_specs = [rows(d), rows(da)] + [mod] * 5 + [_const_spec((SUBLANES, LANES)), _const_spec((tm, tm))]
    args = [x, attn, *mods, counts0, strictly_lower]
    if not prompt:
        in_specs += [rows(dc), rows(dc)]
        args += list(conv_hist)
    weights = [w["w_rest"], w["w_conv"], w["w_pa"], w["w_pb"], w["w_o"], w["ln1_g"], w["ln1_b"],
               w["w_router_t"], w["b_router"]]
    in_specs += [_const_spec(a.shape) for a in weights]
    args += weights
    assert d == SUBLANES * LANES
    out_specs = [rows(d), pl.BlockSpec((tm * SUBLANES, LANES), lambda i: (i, 0)),
                 rows(LANES), rows(LANES), rows(LANES), _const_spec((SUBLANES, LANES))]
    out_shape = [jax.ShapeDtypeStruct((n, d), F32),
                 jax.ShapeDtypeStruct((n * SUBLANES, LANES), F32),
                 jax.ShapeDtypeStruct((n, LANES), jnp.int32),
                 jax.ShapeDtypeStruct((n, LANES), F32),
                 jax.ShapeDtypeStruct((n, LANES), jnp.int32),
                 jax.ShapeDtypeStruct((SUBLANES, LANES), F32)]
    scratch = [pltpu.VMEM((SUBLANES, LANES), F32)]
    if prompt:
        out_specs.append(pl.BlockSpec((1, SUBLANES, dc), lambda i: (i // tiles_per_seq, 0, 0)))
        out_shape.append(jax.ShapeDtypeStruct((n_seq, SUBLANES, dc), F32))
        scratch.append(pltpu.VMEM((SUBLANES, dc), F32))
    else:
        out_specs.append(rows(dc))
        out_shape.append(jax.ShapeDtypeStruct((n, dc), F32))

    def body(*refs):
        _post_kernel(pl.program_id(0), *refs, prompt=prompt, tiles_per_seq=tiles_per_seq, t_new=t_new,
                     alpha=alpha, d_conv=dc, n_experts=n_exp)

    return pl.pallas_call(
        body,
        grid=(n // tm,),
        in_specs=in_specs,
        out_specs=out_specs,
        out_shape=out_shape,
        scratch_shapes=scratch,
        compiler_params=_params("arbitrary"),
        name="post_prompt" if prompt else "post_sample",
    )(*args)


def _count_le(sorted_vals, x):
    return jnp.sum((sorted_vals[None, :] <= x[:, None]).astype(jnp.int32), axis=1)


def _lookup(table, idx):
    hit = idx[:, None] == jnp.arange(table.shape[0], dtype=jnp.int32)[None, :]
    return jnp.sum(jnp.where(hit, table[None, :], 0), axis=1)


def _moe_plan(counts, n_assign):
    n_exp = counts.shape[0]
    assert n_assign % MOE_BLOCK == 0
    n_tiles = n_assign // MOE_BLOCK
    end = jnp.cumsum(counts)
    start = end - counts
    tile_lo = jnp.arange(n_tiles, dtype=jnp.int32) * MOE_BLOCK
    e_lo = jnp.minimum(_count_le(end, tile_lo), n_exp - 1)
    e_hi = jnp.minimum(_count_le(end, tile_lo + (MOE_BLOCK - 1)), n_exp - 1)
    n_vis = e_hi - e_lo + 1
    v_end = jnp.cumsum(n_vis)
    v_start = v_end - n_vis
    n_visits = n_tiles + n_exp - 1
    v = jnp.arange(n_visits, dtype=jnp.int32)
    used = v < v_end[-1]
    tile_v = jnp.minimum(_count_le(v_end, v), n_tiles - 1)
    first_v = used & (v == _lookup(v_start, tile_v))
    expert_v = jnp.where(used, _lookup(e_lo, tile_v) + v - _lookup(v_start, tile_v), e_hi[-1])
    base = tile_v * MOE_BLOCK
    lo_v = jnp.where(used, jnp.clip(_lookup(start, expert_v) - base, 0, MOE_BLOCK), 0)
    hi_v = jnp.where(used, jnp.clip(_lookup(end, expert_v) - base, 0, MOE_BLOCK), 0)
    as_i32 = lambda a: a.astype(jnp.int32)
    return as_i32(start), as_i32(tile_v), as_i32(expert_v), as_i32(lo_v), as_i32(hi_v), as_i32(first_v)


def _rows_to_tiles(ref, x):
    m, d = x.shape
    per = d // LANES
    for s in range(per):
        ref[pl.ds(s, m, stride=per), :] = x[:, s * LANES:(s + 1) * LANES]


def _tiles_to_rows(ref, m, lead=()):
    per = ref.shape[-2] // m
    return jnp.concatenate([ref[(*lead, pl.ds(s, m, stride=per), slice(None))] for s in range(per)], axis=1)


def _for_assignments(tm, fn):
    def row(r, carry):
        for k in range(TOP_K):
            fn(r, k, r * TOP_K + k)
        return carry
    lax.fori_loop(0, tm, row, 0, unroll=2)


def _tile_of(ref, t, lead=()):
    return ref.at[(*lead, pl.ds(pl.multiple_of(t, SUBLANES), SUBLANES))]


def _dispatch_kernel(dst_ref, hp_ref, hs_ref, xb_hbm, sem, *, n_tiles_p, tm):
    i = pl.program_id(0)

    def send_tile(h_ref):
        def copy(r, k, a):
            return pltpu.make_async_copy(_tile_of(h_ref, r * SUBLANES), _tile_of(xb_hbm, dst_ref[a]), sem)
        _for_assignments(tm, lambda r, k, a: copy(r, k, a).start())
        _for_assignments(tm, lambda r, k, a: copy(r, k, a).wait())

    @pl.when(i < n_tiles_p)
    def _():
        send_tile(hp_ref)

    @pl.when(i >= n_tiles_p)
    def _():
        send_tile(hs_ref)


def _dispatch(h2_p, h2_s, dst_rows, tm):
    per = SUBLANES
    n_tiles_p = h2_p.shape[0] // (tm * per)
    n_tiles_s = h2_s.shape[0] // (tm * per)
    flat = pl.BlockSpec((tm * TOP_K,), lambda i: (i,), memory_space=pltpu.SMEM)
    return pl.pallas_call(
        functools.partial(_dispatch_kernel, n_tiles_p=n_tiles_p, tm=tm),
        grid=(n_tiles_p + n_tiles_s,),
        in_specs=[flat,
                  pl.BlockSpec((tm * per, LANES), lambda i: (jnp.minimum(i, n_tiles_p - 1), 0)),
                  pl.BlockSpec((tm * per, LANES), lambda i: (jnp.maximum(i - n_tiles_p, 0), 0))],
        out_specs=pl.BlockSpec(memory_space=pl.ANY),
        scratch_shapes=[pltpu.SemaphoreType.DMA(())],
        out_shape=jax.ShapeDtypeStruct((dst_rows.shape[0] * per, LANES), F32),
        compiler_params=_params("arbitrary"),
        name="dispatch",
    )(dst_rows, h2_p, h2_s)


def _moe_kernel(tile_ref, exp_ref, lo_ref, hi_ref, first_ref, x_ref, wg_ref, bg_ref, wu_ref, bu_ref,
                wd_ref, bd_ref, o_ref):
    v = pl.program_id(0)
    lo = lo_ref[v]
    hi = hi_ref[v]

    @pl.when(hi > lo)
    def _():
        x = _tiles_to_rows(x_ref, MOE_BLOCK).astype(BF16)
        g = jnp.minimum(jnp.dot(x, wg_ref[0], preferred_element_type=F32) + bg_ref[0], SWIGLU_LIMIT)
        u = jnp.clip(jnp.dot(x, wu_ref[0], preferred_element_type=F32) + bu_ref[0], -SWIGLU_LIMIT, SWIGLU_LIMIT)
        act = g * jax.nn.sigmoid(SWIGLU_ALPHA * g) * (u + 1.0)
        y = jnp.dot(act.astype(BF16), wd_ref[0], preferred_element_type=F32) + bd_ref[0]
        row = lax.broadcasted_iota(jnp.int32, y.shape, 0)
        y = jnp.where((row >= lo) & (row < hi), y, 0.0)

        @pl.when(first_ref[v] == 1)
        def _():
            _rows_to_tiles(o_ref, y)

        @pl.when(first_ref[v] == 0)
        def _():
            _rows_to_tiles(o_ref, y + _tiles_to_rows(o_ref, MOE_BLOCK))


def _moe(xb, plan, wg, bg, wu, bu, wd, bd):
    _, tile_v, expert_v, lo_v, hi_v, first_v = plan
    n_exp, d, de = wg.shape
    assert d == SUBLANES * LANES
    tile = pl.BlockSpec((MOE_BLOCK * SUBLANES, LANES), lambda v, t, e, *_: (t[v], 0))
    wspec = lambda a, b: pl.BlockSpec((1, a, b), lambda v, t, e, *_: (e[v], 0, 0))
    return pl.pallas_call(
        _moe_kernel,
        grid_spec=pltpu.PrefetchScalarGridSpec(
            num_scalar_prefetch=5,
            grid=(tile_v.shape[0],),
            in_specs=[tile, wspec(d, de), wspec(1, de), wspec(d, de), wspec(1, de), wspec(de, d), wspec(1, d)],
            out_specs=tile,
        ),
        out_shape=jax.ShapeDtypeStruct(xb.shape, F32),
        compiler_params=_params("arbitrary"),
        name="moe",
    )(tile_v, expert_v, lo_v, hi_v, first_v, xb, wg, bg.reshape(n_exp, 1, de), wu, bu.reshape(n_exp, 1, de),
      wd, bd.reshape(n_exp, 1, d))


def _final_kernel(src_ref, src_next_ref, x1_ref, p_ref, g2_ref, lng_ref, lnb_ref, yb_hbm, o_ref, ybuf, sems,
                  *, alpha, tm):
    i = pl.program_id(0)
    slot = i % 2

    def copy(src, s, r, k):
        return pltpu.make_async_copy(_tile_of(yb_hbm, src), _tile_of(ybuf, r * SUBLANES, (s, k)), sems.at[s])

    def fetch(src_r, s):
        _for_assignments(tm, lambda r, k, a: copy(src_r[a], s, r, k).start())

    @pl.when(i == 0)
    def _():
        fetch(src_ref, 0)

    @pl.when(i + 1 < pl.num_programs(0))
    def _():
        fetch(src_next_ref, 1 - slot)

    _for_assignments(tm, lambda r, k, a: copy(0, slot, r, k).wait())
    p = p_ref[...]
    ys = [_tiles_to_rows(ybuf, tm, (slot, k)) for k in range(TOP_K)]
    y = (p[:, 0:1] * ys[0] + p[:, 1:2] * ys[1]) + (p[:, 2:3] * ys[2] + p[:, 3:4] * ys[3])
    o_ref[...] = _layer_norm(alpha * x1_ref[...] + (1.0 + g2_ref[0]) * y) * lng_ref[...] + lnb_ref[...]


def _final(x1, probs, gate2, yb, src_rows, ln_g, ln_b, *, prompt, n_seq, tm, alpha, row_offset):
    n, d = x1.shape
    n_tiles = n // tm
    assert row_offset % tm == 0
    off = row_offset // tm
    if prompt:
        tiles_per_seq = (n // n_seq) // tm
        mod = pl.BlockSpec((1, 1, d), lambda i: (i // tiles_per_seq, 0, 0))
    else:
        mod = pl.BlockSpec((1, tm, d), lambda i: (i, 0, 0))
    rows = lambda c: pl.BlockSpec((tm, c), lambda i: (i, 0))
    flat = pl.BlockSpec((tm * TOP_K,), lambda i: (off + i,), memory_space=pltpu.SMEM)
    flat_next = pl.BlockSpec((tm * TOP_K,), lambda i: (off + jnp.minimum(i + 1, n_tiles - 1),),
                             memory_space=pltpu.SMEM)
    return pl.pallas_call(
        functools.partial(_final_kernel, alpha=alpha, tm=tm),
        grid=(n_tiles,),
        in_specs=[flat, flat_next, rows(d), rows(LANES), mod, _const_spec((1, d)), _const_spec((1, d)),
                  pl.BlockSpec(memory_space=pl.ANY)],
        out_specs=rows(d),
        scratch_shapes=[pltpu.VMEM((2, TOP_K, tm * SUBLANES, LANES), F32), pltpu.SemaphoreType.DMA((2,))],
        out_shape=jax.ShapeDtypeStruct((n, d), F32),
        compiler_params=_params("arbitrary"),
        name="final_prompt" if prompt else "final_sample",
    )(src_rows, src_rows, x1, probs, gate2, ln_g, ln_b, yb)


def _pick(n, candidates):
    for c in candidates:
        if n % c == 0:
            return c
    raise ValueError(f"no tile size in {candidates} divides {n}")


def _decoder_layer(xp, xs, c_all, cache_kt, cache_vt, conv_state, page_table, lw, *, n_p, n_s, t_new, depth):
    d = xp.shape[1]
    s_len = xp.shape[0] // n_p
    n_rows_s = xs.shape[0]
    alpha = (2 * depth) ** 0.25
    d_attn = lw["w_pa"].shape[0]
    d_conv = lw["w_conv"].shape[1]
    n_exp = lw["w_router"].shape[1]

    ada = _ada(c_all, lw["w_ada"], lw["b_ada"])
    mods = [ada[:, k * d:(k + 1) * d] for k in range(6)]
    tm_s = _pick(n_rows_s, (256, 128, 64, 32, 16, 8))
    mods_p = [m[:n_p].reshape(n_p, 1, d) for m in mods]
    mods_s = [jnp.repeat(m[n_p:], t_new, axis=0).reshape(n_rows_s // tm_s, tm_s, d) for m in mods]

    w_in = lw["w_in"]
    wq = w_in[:, :d_attn].astype(BF16)
    wk = w_in[:, d_attn:2 * d_attn].astype(BF16)
    wv = w_in[:, 2 * d_attn:3 * d_attn].astype(BF16)
    weights = {
        "w_rest": w_in[:, 3 * d_attn:].astype(BF16),
        "w_conv": lw["w_conv"],
        "w_pa": lw["w_pa"].astype(BF16), "w_pb": lw["w_pb"].astype(BF16), "w_o": lw["w_o"].astype(BF16),
        "ln1_g": lw["ln1_g"].reshape(1, d), "ln1_b": lw["ln1_b"].reshape(1, d),
        "w_router_t": lw["w_router"].T, "b_router": lw["b_router"].reshape(1, n_exp),
    }

    tm_p = _pick(s_len, (512, 256, 128))
    q_p, kt_p, ktb_p, vt_p, vb_p = _qkv_prompt(xp, mods_p[0], mods_p[1], wq, wk.T, wv.T, wv, n_p, tm_p)
    tq = _pick(s_len, (512, 256))
    attn_p = _attn_prompt(q_p, ktb_p, vb_p, lw["sb_bias"], n_p, tq, 256)

    q_s, k_s, v_s = _qkv_sample(xs, mods_s[0], mods_s[1], wq, wk, wv, tm_s)
    pages_per_step = _pick(page_table.shape[1], (8, 4, 2, 1))
    attn_s = _attn_sample(q_s, k_s, v_s, lw["sb_bias"], cache_kt, cache_vt, page_table, t_new,
                          pages_per_step).astype(BF16)

    n_tok_p = xp.shape[0]
    tm_post = tm_s
    assert s_len % tm_post == 0
    x1_p, h2_p, idx_p, pr_p, rank_p, counts_p, tail_p = _post(
        xp, attn_p, mods_p[:5], None, jnp.zeros((SUBLANES, LANES), F32), weights,
        prompt=True, n_seq=n_p, t_new=t_new, tm=tm_post, alpha=alpha)
    tpos = jnp.arange(n_rows_s, dtype=jnp.int32) % t_new
    prev = conv_state
    prev_rep = jnp.repeat(prev, t_new, axis=0)
    p1 = jnp.where((tpos == 0)[:, None], prev_rep[:, 1], 0.0)
    p2 = jnp.where((tpos == 0)[:, None], prev_rep[:, 0], jnp.where((tpos == 1)[:, None], prev_rep[:, 1], 0.0))
    x1_s, h2_s, idx_s, pr_s, rank_s, counts_all, cu_s = _post(
        xs, attn_s, mods_s[:5], (p1, p2), counts_p, weights,
        prompt=False, n_seq=n_s, t_new=t_new, tm=tm_s, alpha=alpha)

    idx_flat = jnp.concatenate([idx_p[:, :TOP_K], idx_s[:, :TOP_K]], axis=0).reshape(-1)
    rank_flat = jnp.concatenate([rank_p[:, :TOP_K], rank_s[:, :TOP_K]], axis=0).reshape(-1)
    counts = counts_all[0, :n_exp].astype(jnp.int32)
    plan = _moe_plan(counts, idx_flat.shape[0])
    sorted_rows = (_lookup(plan[0], idx_flat) + rank_flat) * SUBLANES
    xb = _dispatch(h2_p, h2_s, sorted_rows, tm_post)
    yb = _moe(xb, plan, lw["w_gate"].astype(BF16), lw["b_gate"], lw["w_up"].astype(BF16), lw["b_up"],
              lw["w_down"].astype(BF16), lw["b_down"])

    ln2_g = lw["ln2_g"].reshape(1, d)
    ln2_b = lw["ln2_b"].reshape(1, d)
    out_p = _final(x1_p, pr_p, mods_p[5], yb, sorted_rows, ln2_g, ln2_b,
                   prompt=True, n_seq=n_p, tm=tm_post, alpha=alpha, row_offset=0)
    out_s = _final(x1_s, pr_s, mods_s[5], yb, sorted_rows, ln2_g, ln2_b,
                   prompt=False, n_seq=n_s, tm=tm_s, alpha=alpha, row_offset=n_tok_p)

    n_heads = d_attn // HEAD_DIM
    k_prompt = kt_p.reshape(n_p, n_heads, HEAD_DIM, s_len).transpose(0, 3, 1, 2)
    v_prompt = vt_p.reshape(n_p, n_heads, HEAD_DIM, s_len).transpose(0, 3, 1, 2)
    conv_prompt = tail_p[:, SUBLANES - 2:, :]
    k_sample = k_s.reshape(n_s, t_new, n_heads, HEAD_DIM)
    v_sample = v_s.reshape(n_s, t_new, n_heads, HEAD_DIM)
    conv_sample = cu_s.reshape(n_s, t_new, d_conv)[:, t_new - 2:, :]
    return out_p, out_s, k_prompt, v_prompt, conv_prompt, k_sample, v_sample, conv_sample


def kernel(x_prompt, x_sample, cache_k, cache_v, state_conv, page_table, c_prompt, c_sample, w_ada, b_ada, w_in, sb_bias, w_conv, w_pa, w_pb, w_o, ln1_g, ln1_b, w_router, b_router, w_gate, b_gate, w_up, b_up, w_down, b_down, ln2_g, ln2_b):
    depth = w_in.shape[0]
    n_p, s_len, d = x_prompt.shape
    n_s, t_new, _ = x_sample.shape
    assert t_new >= 2 and state_conv.shape[2] == 2
    n_pool, page, n_heads, head_dim = cache_k.shape[1:]
    assert head_dim == HEAD_DIM
    names = ("w_ada", "b_ada", "w_in", "sb_bias", "w_conv", "w_pa", "w_pb", "w_o", "ln1_g", "ln1_b",
             "w_router", "b_router", "w_gate", "b_gate", "w_up", "b_up", "w_down", "b_down", "ln2_g", "ln2_b")
    stacked = (w_ada, b_ada, w_in, sb_bias, w_conv, w_pa, w_pb, w_o, ln1_g, ln1_b,
               w_router, b_router, w_gate, b_gate, w_up, b_up, w_down, b_down, ln2_g, ln2_b)
    xp = x_prompt.reshape(n_p * s_len, d)
    xs = x_sample.reshape(n_s * t_new, d)
    c_all = jnp.concatenate([c_prompt, c_sample], axis=0)
    outs = [[] for _ in range(6)]
    for l in range(depth):
        lw = {n: a[l] for n, a in zip(names, stacked)}
        cache_kt = cache_k[l].transpose(0, 2, 3, 1).reshape(n_pool, n_heads * head_dim, page)
        cache_vt = cache_v[l].transpose(0, 2, 3, 1).reshape(n_pool, n_heads * head_dim, page)
        xp, xs, *rest = _decoder_layer(xp, xs, c_all, cache_kt, cache_vt, state_conv[l], page_table, lw,
                                       n_p=n_p, n_s=n_s, t_new=t_new, depth=depth)
        for o, r in zip(outs, rest):
            o.append(r)
    stack = [jnp.stack(o) for o in outs]
    return (xp.reshape(n_p, s_len, d), xs.reshape(n_s, t_new, d), *stack)
```

```python
import functools

import jax
import jax.numpy as jnp
from jax import lax
from jax.experimental import pallas as pl
from jax.experimental.pallas import tpu as pltpu

F32 = jnp.float32
BF16 = jnp.bfloat16

HEAD_DIM = 64
TOP_K = 4
MOE_BLOCK = 256
SWIGLU_LIMIT = 7.0
SWIGLU_ALPHA = 1.702
LN_EPS = 1e-5

LANES = 128
SUBLANES = 8
HEADS_PER_GROUP = LANES // HEAD_DIM
VMEM_LIMIT = 56 * 1024 * 1024

_NT = (((1,), (1,)), ((), ()))


def _params(*sem):
    return pltpu.CompilerParams(dimension_semantics=sem, vmem_limit_bytes=VMEM_LIMIT)


def _layer_norm(x):
    mu = jnp.mean(x, axis=-1, keepdims=True)
    xc = x - mu
    var = jnp.mean(xc * xc, axis=-1, keepdims=True)
    return xc * lax.rsqrt(var + LN_EPS)


def _softplus(z):
    neg_abs = lax.bitcast_convert_type(lax.bitcast_convert_type(z, jnp.int32) | jnp.int32(-2 ** 31), F32)
    return jnp.maximum(z, 0.0) + jnp.log(1.0 + jnp.exp(neg_abs))


def _const_spec(shape):
    return pl.BlockSpec(shape, lambda *_: (0,) * len(shape))


def _ada_kernel(c_ref, w_ref, b_ref, o_ref):
    o_ref[...] = jnp.dot(c_ref[...], w_ref[...], preferred_element_type=F32,
                         precision=lax.Precision.HIGHEST) + b_ref[...]


def _ada(c, w, b):
    m, d = c.shape
    n = w.shape[1]
    tn = n // 6
    return pl.pallas_call(
        _ada_kernel,
        grid=(n // tn,),
        in_specs=[_const_spec((m, d)),
                  pl.BlockSpec((d, tn), lambda j: (0, j)),
                  pl.BlockSpec((1, tn), lambda j: (0, j))],
        out_specs=pl.BlockSpec((m, tn), lambda j: (0, j)),
        out_shape=jax.ShapeDtypeStruct((m, n), F32),
        compiler_params=_params("arbitrary"),
        name="ada",
    )(c, w, b.reshape(1, n))


def _qkv_prompt_kernel(x_ref, shift_ref, scale_ref, wq_ref, wkt_ref, wvt_ref, wv_ref,
                       q_ref, kt_ref, ktb_ref, vt_ref, vb_ref, *, q_scale):
    h = (_layer_norm(x_ref[...]) * (1.0 + scale_ref[0]) + shift_ref[0]).astype(BF16)
    q = jnp.dot(h, wq_ref[...], preferred_element_type=F32)
    q_ref[...] = (q * q_scale).astype(BF16)
    kt = lax.dot_general(wkt_ref[...], h, _NT, preferred_element_type=F32)
    kt_ref[0] = kt
    ktb_ref[0] = kt.astype(BF16)
    vt_ref[0] = lax.dot_general(wvt_ref[...], h, _NT, preferred_element_type=F32)
    vb_ref[...] = jnp.dot(h, wv_ref[...], preferred_element_type=F32).astype(BF16)


def _qkv_prompt(x, shift, scale, wq, wkt, wvt, wv, n_seq, tm):
    n, d = x.shape
    da = wq.shape[1]
    s_len = n // n_seq
    tps = s_len // tm
    mod = pl.BlockSpec((1, 1, d), lambda i: (i // tps, 0, 0))
    row = pl.BlockSpec((tm, da), lambda i: (i, 0))
    tr = pl.BlockSpec((1, da, tm), lambda i: (i // tps, 0, i % tps))
    return pl.pallas_call(
        functools.partial(_qkv_prompt_kernel, q_scale=HEAD_DIM ** -0.5),
        grid=(n // tm,),
        in_specs=[pl.BlockSpec((tm, d), lambda i: (i, 0)), mod, mod,
                  _const_spec((d, da)), _const_spec((da, d)), _const_spec((da, d)), _const_spec((d, da))],
        out_specs=[row, tr, tr, tr, row],
        out_shape=[jax.ShapeDtypeStruct((n, da), BF16),
                   jax.ShapeDtypeStruct((n_seq, da, s_len), F32),
                   jax.ShapeDtypeStruct((n_seq, da, s_len), BF16),
                   jax.ShapeDtypeStruct((n_seq, da, s_len), F32),
                   jax.ShapeDtypeStruct((n, da), BF16)],
        compiler_params=_params("arbitrary"),
        name="qkv_prompt",
    )(x, shift, scale, wq, wkt, wvt, wv)


def _qkv_sample_kernel(x_ref, shift_ref, scale_ref, wq_ref, wk_ref, wv_ref, q_ref, k_ref, v_ref, *, q_scale):
    h = (_layer_norm(x_ref[...]) * (1.0 + scale_ref[0]) + shift_ref[0]).astype(BF16)
    q_ref[...] = (jnp.dot(h, wq_ref[...], preferred_element_type=F32) * q_scale).astype(BF16)
    k_ref[...] = jnp.dot(h, wk_ref[...], preferred_element_type=F32)
    v_ref[...] = jnp.dot(h, wv_ref[...], preferred_element_type=F32)


def _qkv_sample(x, shift, scale, wq, wk, wv, tm):
    n, d = x.shape
    da = wq.shape[1]
    mod = pl.BlockSpec((1, tm, d), lambda i: (i, 0, 0))
    row = pl.BlockSpec((tm, da), lambda i: (i, 0))
    return pl.pallas_call(
        functools.partial(_qkv_sample_kernel, q_scale=HEAD_DIM ** -0.5),
        grid=(n // tm,),
        in_specs=[pl.BlockSpec((tm, d), lambda i: (i, 0)), mod, mod,
                  _const_spec((d, da)), _const_spec((d, da)), _const_spec((d, da))],
        out_specs=[row, row, row],
        out_shape=[jax.ShapeDtypeStruct((n, da), BF16),
                   jax.ShapeDtypeStruct((n, da), F32),
                   jax.ShapeDtypeStruct((n, da), F32)],
        compiler_params=_params("arbitrary"),
        name="qkv_sample",
    )(x, shift, scale, wq, wk, wv)


def _attn_prompt_kernel(bias_ref, q_ref, kt_ref, v_ref, u_ref, o_ref, *, tq, tk):
    g = pl.program_id(1)
    qi = pl.program_id(2)
    q2 = q_ref[...]
    lane = lax.broadcasted_iota(jnp.int32, (tq, LANES), 1)
    n_diag = tq // tk
    n_off = qi * n_diag
    row = lax.broadcasted_iota(jnp.int32, (tq, tk), 0)
    col = lax.broadcasted_iota(jnp.int32, (tq, tk), 1)
    u = u_ref[...]
    heads = range(HEADS_PER_GROUP)
    qs = [jnp.where(lane // HEAD_DIM == p, q2, jnp.zeros_like(q2)) for p in heads]
    biases = [bias_ref[g * HEADS_PER_GROUP + p] for p in heads]

    def sub_block(off, state, vis):
        kt = kt_ref[0, :, pl.ds(off, tk)]
        v = v_ref[pl.ds(off, tk), :]
        new_state = []
        for qp, bias, (o, c) in zip(qs, biases, state):
            z = jnp.dot(qp, kt, preferred_element_type=F32) + bias
            sp = _softplus(z)
            if vis is not None:
                sp = jnp.where(vis, sp, 0.0)
            r = jnp.dot(sp.astype(BF16), u, preferred_element_type=F32)
            a = jnp.exp(z - r[:, :tk] - jnp.concatenate([c] * (tk // LANES), axis=1))
            if vis is not None:
                a = jnp.where(vis, a, 0.0)
            o = o + jnp.dot(a.astype(BF16), v, preferred_element_type=F32)
            new_state.append((o, c + r[:, tk:]))
        return tuple(new_state)

    def key_block(kb, state, masked):
        for d in reversed(range(n_diag)):
            off = pl.multiple_of(kb * tq + d * tk, tk)
            state = sub_block(off, state, (d * tk + col < row) if masked else None)
        return state

    zeros = jnp.zeros((tq, LANES), F32)
    state = key_block(qi, tuple((zeros, zeros) for _ in heads), True)
    state = lax.fori_loop(0, qi, lambda j, st: key_block(qi - 1 - j, st, False), state)
    o_ref[...] = jnp.where(lane < HEAD_DIM, state[0][0], state[1][0]).astype(o_ref.dtype)


def _suffix_ones(n, extra=0):
    j = lax.broadcasted_iota(jnp.int32, (n, n + extra), 0)
    s = lax.broadcasted_iota(jnp.int32, (n, n + extra), 1)
    return ((j >= s) | (s >= n)).astype(BF16)


def _attn_prompt(q, ktb, vb, sb_bias, n_seq, tq, tk):
    n, da = q.shape
    s_len = n // n_seq
    nq = s_len // tq
    n_groups = da // LANES
    return pl.pallas_call(
        functools.partial(_attn_prompt_kernel, tq=tq, tk=tk),
        grid_spec=pltpu.PrefetchScalarGridSpec(
            num_scalar_prefetch=1,
            grid=(n_seq, n_groups, nq),
            in_specs=[pl.BlockSpec((tq, LANES), lambda b, g, i, *_: (b * nq + i, g)),
                      pl.BlockSpec((1, LANES, s_len), lambda b, g, i, *_: (b, g, 0)),
                      pl.BlockSpec((s_len, LANES), lambda b, g, i, *_: (b, g)),
                      pl.BlockSpec((tk, tk + LANES), lambda b, g, i, *_: (0, 0))],
            out_specs=pl.BlockSpec((tq, LANES), lambda b, g, i, *_: (b * nq + i, g)),
        ),
        out_shape=jax.ShapeDtypeStruct((n, da), BF16),
        compiler_params=_params("arbitrary", "arbitrary", "arbitrary"),
        name="attn_prompt",
    )(sb_bias, q, ktb, vb, _suffix_ones(tk, LANES))


def _attn_sample_kernel(pt_ref, qbd_ref, bias_ref, knew_ref, vnew_ref, u_ref, ck_hbm, cv_hbm,
                        o_ref, acc_ref, carry_ref, kbuf, vbuf, kpages, vpages, sems,
                        *, pages_per_step, n_pages, n_heads, t_new):
    b = pl.program_id(0)
    j = pl.program_id(1)
    n_steps = pl.num_programs(1)
    step = b * n_steps + j
    slot = step % 2
    n_rows = t_new * n_heads
    page = kbuf.shape[1]
    qbd = qbd_ref[0]
    bias = bias_ref[...]
    u = u_ref[...]

    def page_copies(seq, jj, s):
        copies = []
        for i in range(pages_per_step):
            pid = pt_ref[seq * n_pages + (n_pages - 1 - (jj * pages_per_step + i))]
            copies.append(pltpu.make_async_copy(ck_hbm.at[pid], kpages.at[s, i], sems.at[0, s]))
            copies.append(pltpu.make_async_copy(cv_hbm.at[pid], vpages.at[s, i], sems.at[1, s]))
        return copies

    @pl.when(step == 0)
    def _():
        for cp in page_copies(0, 0, 0):
            cp.start()

    @pl.when(step + 1 < pl.num_programs(0) * n_steps)
    def _():
        wrap = j + 1 == n_steps
        for cp in page_copies(jnp.where(wrap, b + 1, b), jnp.where(wrap, 0, j + 1), 1 - slot):
            cp.start()

    def blocks(kts, vts, vis, o, c):
        zs = [jnp.dot(qbd, kt.astype(BF16), preferred_element_type=F32) + bias for kt in kts]
        sps = [_softplus(z) for z in zs]
        if vis is not None:
            sps = [jnp.where(vis, sp, 0.0) for sp in sps]
        rs = [jnp.dot(sp.astype(BF16), u, preferred_element_type=F32) for sp in sps]
        for z, r, vt in zip(zs, rs, vts):
            a = jnp.exp(z - r[:, :page] - jnp.concatenate([c] * (page // LANES), axis=1))
            if vis is not None:
                a = jnp.where(vis, a, 0.0)
            o = o + lax.dot_general(a.astype(BF16), vt.astype(BF16), _NT, preferred_element_type=F32)
            c = c + r[:, page:]
        return o, c

    @pl.when(j == 0)
    def _():
        kbuf[...] = jnp.zeros(kbuf.shape, F32)
        vbuf[...] = jnp.zeros(vbuf.shape, F32)
        kbuf[:, 0:t_new] = knew_ref[0]
        vbuf[:, 0:t_new] = vnew_ref[0]
        row_t = lax.broadcasted_iota(jnp.int32, (n_rows, page), 0) // n_heads
        col = lax.broadcasted_iota(jnp.int32, (n_rows, page), 1)
        o, c = blocks([kbuf[...]], [vbuf[...]], col < row_t,
                      jnp.zeros(acc_ref.shape, F32), jnp.zeros(carry_ref.shape, F32))
        acc_ref[...] = o
        carry_ref[...] = c

    for cp in page_copies(b, j, slot):
        cp.wait()
    o, c = blocks([kpages[slot, i] for i in range(pages_per_step)],
                  [vpages[slot, i] for i in range(pages_per_step)], None, acc_ref[...], carry_ref[...])
    acc_ref[...] = o
    carry_ref[...] = c

    @pl.when(j == pl.num_programs(1) - 1)
    def _():
        r = lax.broadcasted_iota(jnp.int32, o.shape, 0)
        l = lax.broadcasted_iota(jnp.int32, o.shape, 1)
        own = jnp.where(l // HEAD_DIM == r % n_heads, o, 0.0)
        o_ref[0] = own.reshape(t_new, n_heads, o.shape[1]).sum(axis=1)


def _attn_sample(q, k_new, v_new, sb_bias, cache_kt, cache_vt, page_table, t_new, pages_per_step):
    n, da = q.shape
    n_seq = n // t_new
    n_heads = da // HEAD_DIM
    n_pages = page_table.shape[1]
    page = cache_kt.shape[2]
    n_rows = t_new * n_heads
    assert n_pages % pages_per_step == 0 and t_new <= page
    head_of_col = jnp.arange(da, dtype=jnp.int32) // HEAD_DIM
    blockdiag = head_of_col[None, :] == jnp.arange(n_heads, dtype=jnp.int32)[:, None]
    qbd = jnp.where(blockdiag[None, None], q.reshape(n_seq, t_new, 1, da), jnp.zeros((), q.dtype))
    qbd = qbd.reshape(n_seq, n_rows, da)
    assert page % LANES == 0
    bias_rows = jnp.broadcast_to(jnp.tile(sb_bias.astype(F32), t_new).reshape(n_rows, 1), (n_rows, page))
    knew_t = k_new.reshape(n_seq, t_new, da).transpose(0, 2, 1)
    vnew_t = v_new.reshape(n_seq, t_new, da).transpose(0, 2, 1)

    seq3 = lambda b, j, pt: (b, 0, 0)
    const2 = lambda b, j, pt: (0, 0)
    specs = [pl.BlockSpec((1, n_rows, da), seq3),
             pl.BlockSpec((n_rows, page), const2),
             pl.BlockSpec((1, da, t_new), seq3),
             pl.BlockSpec((1, da, t_new), seq3),
             pl.BlockSpec((page, page + LANES), const2),
             pl.BlockSpec(memory_space=pl.ANY),
             pl.BlockSpec(memory_space=pl.ANY)]
    page_buf = pltpu.VMEM((2, pages_per_step, da, page), F32)
    out = pl.pallas_call(
        functools.partial(_attn_sample_kernel, pages_per_step=pages_per_step, n_pages=n_pages,
                          n_heads=n_heads, t_new=t_new),
        grid_spec=pltpu.PrefetchScalarGridSpec(
            num_scalar_prefetch=1,
            grid=(n_seq, n_pages // pages_per_step),
            in_specs=specs,
            out_specs=pl.BlockSpec((1, t_new, da), seq3),
            scratch_shapes=[pltpu.VMEM((n_rows, da), F32), pltpu.VMEM((n_rows, LANES), F32),
                            pltpu.VMEM((da, page), F32), pltpu.VMEM((da, page), F32),
                            page_buf, page_buf, pltpu.SemaphoreType.DMA((2, 2))],
        ),
        out_shape=jax.ShapeDtypeStruct((n_seq, t_new, da), F32),
        compiler_params=_params("arbitrary", "arbitrary"),
        name="attn_sample",
    )(page_table.reshape(-1), qbd, bias_rows, knew_t, vnew_t, _suffix_ones(page, LANES), cache_kt, cache_vt)
    return out.reshape(n, da)


def _post_kernel(i, *refs, prompt, tiles_per_seq, t_new, alpha, d_conv, n_experts):
    if prompt:
        (x_ref, attn_ref, sh1, sc1, g1, sh2, sc2, cnt0_ref, lt_ref, wr_ref, wc_ref, wpa_ref, wpb_ref, wo_ref,
         ln1g_ref, ln1b_ref, wrt_ref, br_ref,
         x1_ref, h2_ref, idx_ref, p_ref, rank_ref, cnt_out_ref, cu_ref, cnt_ref, tail_ref) = refs
    else:
        (x_ref, attn_ref, sh1, sc1, g1, sh2, sc2, cnt0_ref, lt_ref, p1_ref, p2_ref,
         wr_ref, wc_ref, wpa_ref, wpb_ref, wo_ref, ln1g_ref, ln1b_ref, wrt_ref, br_ref,
         x1_ref, h2_ref, idx_ref, p_ref, rank_ref, cnt_out_ref, cu_ref, cnt_ref) = refs
    x = x_ref[...]
    tm, d = x.shape
    dc = d_conv
    h = (_layer_norm(x) * (1.0 + sc1[0]) + sh1[0]).astype(BF16)
    y = jnp.dot(h, wr_ref[...], preferred_element_type=F32)
    gate_b = y[:, dc:2 * dc]
    cu = y[:, 2 * dc:3 * dc] * y[:, 0:dc]
    r1 = pltpu.roll(cu, 1, 0)
    r2 = pltpu.roll(cu, 2, 0)
    if prompt:
        first = (i % tiles_per_seq) == 0
        tail = jnp.where(first, 0.0, tail_ref[...])
        row8 = lax.broadcasted_iota(jnp.int32, (SUBLANES, dc), 0)
        f1 = jnp.where(row8 >= 1, r1[0:SUBLANES], pltpu.roll(tail, 1, 0))
        f2 = jnp.where(row8 >= 2, r2[0:SUBLANES], pltpu.roll(tail, 2, 0))
        z1 = jnp.concatenate([f1, r1[SUBLANES:]], axis=0)
        z2 = jnp.concatenate([f2, r2[SUBLANES:]], axis=0)
        tail_ref[...] = cu[tm - SUBLANES:, :]
        cu_ref[0] = cu[tm - SUBLANES:, :]
    else:
        tpos = lax.broadcasted_iota(jnp.int32, (tm, dc), 0) % t_new
        z1 = jnp.where(tpos >= 1, r1, p1_ref[...])
        z2 = jnp.where(tpos >= 2, r2, p2_ref[...])
        cu_ref[...] = cu
    wc = wc_ref[...]
    conv = wc[0:1] * z2 + wc[1:2] * z1 + wc[2:3] * cu
    branch_b = jnp.dot((gate_b * conv).astype(BF16), wpb_ref[...], preferred_element_type=F32)
    branch_a = jnp.dot(attn_ref[...], wpa_ref[...], preferred_element_type=F32)
    merged = (jax.nn.sigmoid(y[:, 3 * dc:3 * dc + d]) * branch_a
              + jax.nn.sigmoid(y[:, 3 * dc + d:]) * branch_b)
    yo = jnp.dot(merged.astype(BF16), wo_ref[...], preferred_element_type=F32)
    x1 = _layer_norm(alpha * x + (1.0 + g1[0]) * yo) * ln1g_ref[...] + ln1b_ref[...]
    x1_ref[...] = x1
    h2 = _layer_norm(x1) * (1.0 + sc2[0]) + sh2[0]
    _rows_to_tiles(h2_ref, h2)
    logits = lax.dot_general(h2, wrt_ref[...], _NT, preferred_element_type=F32,
                             precision=lax.Precision.HIGHEST) + br_ref[...]
    lane_e = lax.broadcasted_iota(jnp.int32, logits.shape, 1)
    lane_o = lax.broadcasted_iota(jnp.int32, (tm, LANES), 1)
    idx_out = jnp.zeros((tm, LANES), jnp.int32)
    e_out = jnp.zeros((tm, LANES), F32)
    denom = jnp.zeros((tm, 1), F32)
    top = None
    sels = []
    for k in range(TOP_K):
        m = jnp.max(logits, axis=-1, keepdims=True)
        sel = jnp.min(jnp.where(logits == m, lane_e, n_experts), axis=-1, keepdims=True)
        logits = jnp.where(lane_e == sel, -jnp.inf, logits)
        if top is None:
            top = m
        e = jnp.exp(m - top)
        denom = denom + e
        idx_out = jnp.where(lane_o == k, sel, idx_out)
        e_out = jnp.where(lane_o == k, e, e_out)
        sels.append(sel)
    idx_ref[...] = idx_out
    p_ref[...] = e_out / denom

    @pl.when(i == 0)
    def _():
        cnt_ref[...] = cnt0_ref[...]

    base = cnt_ref[0:1, :]
    onehot = jnp.zeros((tm, LANES), F32)
    for sel in sels:
        onehot = onehot + (lane_o == sel).astype(F32)
    before = jnp.dot(lt_ref[...], onehot.astype(BF16), preferred_element_type=F32) + base
    rank_out = jnp.zeros((tm, LANES), jnp.int32)
    for k, sel in enumerate(sels):
        rk = jnp.sum(jnp.where(lane_o == sel, before, 0.0), axis=-1, keepdims=True)
        rank_out = jnp.where(lane_o == k, rk.astype(jnp.int32), rank_out)
    rank_ref[...] = rank_out
    cnt = jnp.broadcast_to(base + jnp.sum(onehot, axis=0, keepdims=True), cnt_ref.shape)
    cnt_ref[...] = cnt
    cnt_out_ref[...] = cnt


def _post(x, attn, mods, conv_hist, counts0, w, *, prompt, n_seq, t_new, tm, alpha):
    n, d = x.shape
    da = attn.shape[1]
    dc = w["w_conv"].shape[1]
    n_exp = w["w_router_t"].shape[0]
    assert n_exp <= LANES
    tiles_per_seq = (n // n_seq) // tm if prompt else 0
    if prompt:
        mod = pl.BlockSpec((1, 1, d), lambda i: (i // tiles_per_seq, 0, 0))
    else:
        mod = pl.BlockSpec((1, tm, d), lambda i: (i, 0, 0))
    rows = lambda c: pl.BlockSpec((tm, c), lambda i: (i, 0))
    t_row = lax.broadcasted_iota(jnp.int32, (tm, tm), 0)
    t_col = lax.broadcasted_iota(jnp.int32, (tm, tm), 1)
    strictly_lower = (t_col < t_row).astype(BF16)
    in_specs = [rows(d), rows(da)] + [mod] * 5 + [_const_spec((SUBLANES, LANES)), _const_spec((tm, tm))]
    args = [x, attn, *mods, counts0, strictly_lower]
    if not prompt:
        in_specs += [rows(dc), rows(dc)]
        args += list(conv_hist)
    weights = [w["w_rest"], w["w_conv"], w["w_pa"], w["w_pb"], w["w_o"], w["ln1_g"], w["ln1_b"],
               w["w_router_t"], w["b_router"]]
    in_specs += [_const_spec(a.shape) for a in weights]
    args += weights
    assert d == SUBLANES * LANES
    out_specs = [rows(d), pl.BlockSpec((tm * SUBLANES, LANES), lambda i: (i, 0)),
                 rows(LANES), rows(LANES), rows(LANES), _const_spec((SUBLANES, LANES))]
    out_shape = [jax.ShapeDtypeStruct((n, d), F32),
                 jax.ShapeDtypeStruct((n * SUBLANES, LANES), F32),
                 jax.ShapeDtypeStruct((n, LANES), jnp.int32),
                 jax.ShapeDtypeStruct((n, LANES), F32),
                 jax.ShapeDtypeStruct((n, LANES), jnp.int32),
                 jax.ShapeDtypeStruct((SUBLANES, LANES), F32)]
    scratch = [pltpu.VMEM((SUBLANES, LANES), F32)]
    if prompt:
        out_specs.append(pl.BlockSpec((1, SUBLANES, dc), lambda i: (i // tiles_per_seq, 0, 0)))
        out_shape.append(jax.ShapeDtypeStruct((n_seq, SUBLANES, dc), F32))
        scratch.append(pltpu.VMEM((SUBLANES, dc), F32))
    else:
        out_specs.append(rows(dc))
        out_shape.append(jax.ShapeDtypeStruct((n, dc), F32))

    def body(*refs):
        _post_kernel(pl.program_id(0), *refs, prompt=prompt, tiles_per_seq=tiles_per_seq, t_new=t_new,
                     alpha=alpha, d_conv=dc, n_experts=n_exp)

    return pl.pallas_call(
        body,
        grid=(n // tm,),
        in_specs=in_specs,
        out_specs=out_specs,
        out_shape=out_shape,
        scratch_shapes=scratch,
        compiler_params=_params("arbitrary"),
        name="post_prompt" if prompt else "post_sample",
    )(*args)


def _count_le(sorted_vals, x):
    return jnp.sum((sorted_vals[None, :] <= x[:, None]).astype(jnp.int32), axis=1)


def _lookup(table, idx):
    hit = idx[:, None] == jnp.arange(table.shape[0], dtype=jnp.int32)[None, :]
    return jnp.sum(jnp.where(hit, table[None, :], 0), axis=1)


def _moe_plan(counts, n_assign):
    n_exp = counts.shape[0]
    assert n_assign % MOE_BLOCK == 0
    n_tiles = n_assign // MOE_BLOCK
    end = jnp.cumsum(counts)
    start = end - counts
    tile_lo = jnp.arange(n_tiles, dtype=jnp.int32) * MOE_BLOCK
    e_lo = jnp.minimum(_count_le(end, tile_lo), n_exp - 1)
    e_hi = jnp.minimum(_count_le(end, tile_lo + (MOE_BLOCK - 1)), n_exp - 1)
    n_vis = e_hi - e_lo + 1
    v_end = jnp.cumsum(n_vis)
    v_start = v_end - n_vis
    n_visits = n_tiles + n_exp - 1
    v = jnp.arange(n_visits, dtype=jnp.int32)
    used = v < v_end[-1]
    tile_v = jnp.minimum(_count_le(v_end, v), n_tiles - 1)
    first_v = used & (v == _lookup(v_start, tile_v))
    expert_v = jnp.where(used, _lookup(e_lo, tile_v) + v - _lookup(v_start, tile_v), e_hi[-1])
    base = tile_v * MOE_BLOCK
    lo_v = jnp.where(used, jnp.clip(_lookup(start, expert_v) - base, 0, MOE_BLOCK), 0)
    hi_v = jnp.where(used, jnp.clip(_lookup(end, expert_v) - base, 0, MOE_BLOCK), 0)
    as_i32 = lambda a: a.astype(jnp.int32)
    return as_i32(start), as_i32(tile_v), as_i32(expert_v), as_i32(lo_v), as_i32(hi_v), as_i32(first_v)


def _rows_to_tiles(ref, x):
    m, d = x.shape
    per = d // LANES
    for s in range(per):
        ref[pl.ds(s, m, stride=per), :] = x[:, s * LANES:(s + 1) * LANES]


def _tiles_to_rows(ref, m, lead=()):
    per = ref.shape[-2] // m
    return jnp.concatenate([ref[(*lead, pl.ds(s, m, stride=per), slice(None))] for s in range(per)], axis=1)


def _for_assignments(tm, fn):
    def row(r, carry):
        for k in range(TOP_K):
            fn(r, k, r * TOP_K + k)
        return carry
    lax.fori_loop(0, tm, row, 0, unroll=2)


def _tile_of(ref, t, lead=()):
    return ref.at[(*lead, pl.ds(pl.multiple_of(t, SUBLANES), SUBLANES))]


def _dispatch_kernel(dst_ref, hp_ref, hs_ref, xb_hbm, sem, *, n_tiles_p, tm):
    i = pl.program_id(0)

    def send_tile(h_ref):
        def copy(r, k, a):
            return pltpu.make_async_copy(_tile_of(h_ref, r * SUBLANES), _tile_of(xb_hbm, dst_ref[a]), sem)
        _for_assignments(tm, lambda r, k, a: copy(r, k, a).start(priority=k % 2))
        _for_assignments(tm, lambda r, k, a: copy(r, k, a).wait())

    @pl.when(i < n_tiles_p)
    def _():
        send_tile(hp_ref)

    @pl.when(i >= n_tiles_p)
    def _():
        send_tile(hs_ref)


def _dispatch(h2_p, h2_s, dst_rows, tm):
    per = SUBLANES
    n_tiles_p = h2_p.shape[0] // (tm * per)
    n_tiles_s = h2_s.shape[0] // (tm * per)
    flat = pl.BlockSpec((tm * TOP_K,), lambda i: (i,), memory_space=pltpu.SMEM)
    return pl.pallas_call(
        functools.partial(_dispatch_kernel, n_tiles_p=n_tiles_p, tm=tm),
        grid=(n_tiles_p + n_tiles_s,),
        in_specs=[flat,
                  pl.BlockSpec((tm * per, LANES), lambda i: (jnp.minimum(i, n_tiles_p - 1), 0)),
                  pl.BlockSpec((tm * per, LANES), lambda i: (jnp.maximum(i - n_tiles_p, 0), 0))],
        out_specs=pl.BlockSpec(memory_space=pl.ANY),
        scratch_shapes=[pltpu.SemaphoreType.DMA(())],
        out_shape=jax.ShapeDtypeStruct((dst_rows.shape[0] * per, LANES), F32),
        compiler_params=_params("arbitrary"),
        name="dispatch",
    )(dst_rows, h2_p, h2_s)


def _moe_kernel(tile_ref, exp_ref, lo_ref, hi_ref, first_ref, x_ref, wg_ref, bg_ref, wu_ref, bu_ref,
                wd_ref, bd_ref, o_ref, wg_bf, wu_bf, wd_bf):
    v = pl.program_id(0)
    lo = lo_ref[v]
    hi = hi_ref[v]

    @pl.when((v == 0) | (exp_ref[v] != exp_ref[jnp.maximum(v - 1, 0)]))
    def _():
        wg_bf[...] = wg_ref[0].astype(BF16)
        wu_bf[...] = wu_ref[0].astype(BF16)
        wd_bf[...] = wd_ref[0].astype(BF16)

    @pl.when(hi > lo)
    def _():
        x = _tiles_to_rows(x_ref, MOE_BLOCK).astype(BF16)
        g = jnp.minimum(jnp.dot(x, wg_bf[...], preferred_element_type=F32) + bg_ref[0], SWIGLU_LIMIT)
        u = jnp.clip(jnp.dot(x, wu_bf[...], preferred_element_type=F32) + bu_ref[0], -SWIGLU_LIMIT, SWIGLU_LIMIT)
        act = g * jax.nn.sigmoid(SWIGLU_ALPHA * g) * (u + 1.0)
        y = jnp.dot(act.astype(BF16), wd_bf[...], preferred_element_type=F32) + bd_ref[0]
        row = lax.broadcasted_iota(jnp.int32, y.shape, 0)
        y = jnp.where((row >= lo) & (row < hi), y, 0.0)

        @pl.when(first_ref[v] == 1)
        def _():
            _rows_to_tiles(o_ref, y)

        @pl.when(first_ref[v] == 0)
        def _():
            _rows_to_tiles(o_ref, y + _tiles_to_rows(o_ref, MOE_BLOCK))


def _moe(xb, plan, wg, bg, wu, bu, wd, bd):
    _, tile_v, expert_v, lo_v, hi_v, first_v = plan
    n_exp, d, de = wg.shape
    assert d == SUBLANES * LANES
    tile = pl.BlockSpec((MOE_BLOCK * SUBLANES, LANES), lambda v, t, e, *_: (t[v], 0))
    wspec = lambda a, b: pl.BlockSpec((1, a, b), lambda v, t, e, *_: (e[v], 0, 0))
    return pl.pallas_call(
        _moe_kernel,
        grid_spec=pltpu.PrefetchScalarGridSpec(
            num_scalar_prefetch=5,
            grid=(tile_v.shape[0],),
            in_specs=[tile, wspec(d, de), wspec(1, de), wspec(d, de), wspec(1, de), wspec(de, d), wspec(1, d)],
            out_specs=tile,
            scratch_shapes=[pltpu.VMEM((d, de), BF16), pltpu.VMEM((d, de), BF16), pltpu.VMEM((de, d), BF16)],
        ),
        out_shape=jax.ShapeDtypeStruct(xb.shape, F32),
        compiler_params=_params("arbitrary"),
        name="moe",
    )(tile_v, expert_v, lo_v, hi_v, first_v, xb, wg, bg.reshape(n_exp, 1, de), wu, bu.reshape(n_exp, 1, de),
      wd, bd.reshape(n_exp, 1, d))


def _final_kernel(src_ref, src_next_ref, x1_ref, p_ref, g2_ref, lng_ref, lnb_ref, yb_hbm, o_ref, ybuf, sems,
                  *, alpha, tm):
    i = pl.program_id(0)
    slot = i % 2

    def copy(src, s, r, k):
        return pltpu.make_async_copy(_tile_of(yb_hbm, src), _tile_of(ybuf, r * SUBLANES, (s, k)), sems.at[s])

    def fetch(src_r, s):
        _for_assignments(tm, lambda r, k, a: copy(src_r[a], s, r, k).start(priority=k % 2))

    @pl.when(i == 0)
    def _():
        fetch(src_ref, 0)

    @pl.when(i + 1 < pl.num_programs(0))
    def _():
        fetch(src_next_ref, 1 - slot)

    _for_assignments(tm, lambda r, k, a: copy(0, slot, r, k).wait())
    p = p_ref[...]
    ys = [_tiles_to_rows(ybuf, tm, (slot, k)) for k in range(TOP_K)]
    y = (p[:, 0:1] * ys[0] + p[:, 1:2] * ys[1]) + (p[:, 2:3] * ys[2] + p[:, 3:4] * ys[3])
    o_ref[...] = _layer_norm(alpha * x1_ref[...] + (1.0 + g2_ref[0]) * y) * lng_ref[...] + lnb_ref[...]


def _final(x1, probs, gate2, yb, src_rows, ln_g, ln_b, *, prompt, n_seq, tm, alpha, row_offset):
    n, d = x1.shape
    n_tiles = n // tm
    assert row_offset % tm == 0
    off = row_offset // tm
    if prompt:
        tiles_per_seq = (n // n_seq) // tm
        mod = pl.BlockSpec((1, 1, d), lambda i: (i // tiles_per_seq, 0, 0))
    else:
        mod = pl.BlockSpec((1, tm, d), lambda i: (i, 0, 0))
    rows = lambda c: pl.BlockSpec((tm, c), lambda i: (i, 0))
    flat = pl.BlockSpec((tm * TOP_K,), lambda i: (off + i,), memory_space=pltpu.SMEM)
    flat_next = pl.BlockSpec((tm * TOP_K,), lambda i: (off + jnp.minimum(i + 1, n_tiles - 1),),
                             memory_space=pltpu.SMEM)
    return pl.pallas_call(
        functools.partial(_final_kernel, alpha=alpha, tm=tm),
        grid=(n_tiles,),
        in_specs=[flat, flat_next, rows(d), rows(LANES), mod, _const_spec((1, d)), _const_spec((1, d)),
                  pl.BlockSpec(memory_space=pl.ANY)],
        out_specs=rows(d),
        scratch_shapes=[pltpu.VMEM((2, TOP_K, tm * SUBLANES, LANES), F32), pltpu.SemaphoreType.DMA((2,))],
        out_shape=jax.ShapeDtypeStruct((n, d), F32),
        compiler_params=_params("arbitrary"),
        name="final_prompt" if prompt else "final_sample",
    )(src_rows, src_rows, x1, probs, gate2, ln_g, ln_b, yb)


def _pick(n, candidates):
    for c in candidates:
        if n % c == 0:
            return c
    raise ValueError(f"no tile size in {candidates} divides {n}")


def _decoder_layer(xp, xs, c_all, cache_kt, cache_vt, conv_state, page_table, lw, *, n_p, n_s, t_new, depth):
    d = xp.shape[1]
    s_len = xp.shape[0] // n_p
    n_rows_s = xs.shape[0]
    alpha = (2 * depth) ** 0.25
    d_attn = lw["w_pa"].shape[0]
    d_conv = lw["w_conv"].shape[1]
    n_exp = lw["w_router"].shape[1]

    ada = _ada(c_all, lw["w_ada"], lw["b_ada"])
    mods = [ada[:, k * d:(k + 1) * d] for k in range(6)]
    tm_s = _pick(n_rows_s, (256, 128, 64, 32, 16, 8))
    mods_p = [m[:n_p].reshape(n_p, 1, d) for m in mods]
    mods_s = [jnp.repeat(m[n_p:], t_new, axis=0).reshape(n_rows_s // tm_s, tm_s, d) for m in mods]

    w_in = lw["w_in"]
    wq = w_in[:, :d_attn].astype(BF16)
    wk = w_in[:, d_attn:2 * d_attn].astype(BF16)
    wv = w_in[:, 2 * d_attn:3 * d_attn].astype(BF16)
    weights = {
        "w_rest": w_in[:, 3 * d_attn:].astype(BF16),
        "w_conv": lw["w_conv"],
        "w_pa": lw["w_pa"].astype(BF16), "w_pb": lw["w_pb"].astype(BF16), "w_o": lw["w_o"].astype(BF16),
        "ln1_g": lw["ln1_g"].reshape(1, d), "ln1_b": lw["ln1_b"].reshape(1, d),
        "w_router_t": lw["w_router"].T, "b_router": lw["b_router"].reshape(1, n_exp),
    }

    tm_p = _pick(s_len, (512, 256, 128))
    q_p, kt_p, ktb_p, vt_p, vb_p = _qkv_prompt(xp, mods_p[0], mods_p[1], wq, wk.T, wv.T, wv, n_p, tm_p)
    tq = _pick(s_len, (512, 256))
    attn_p = _attn_prompt(q_p, ktb_p, vb_p, lw["sb_bias"], n_p, tq, 256)

    q_s, k_s, v_s = _qkv_sample(xs, mods_s[0], mods_s[1], wq, wk, wv, tm_s)
    pages_per_step = _pick(page_table.shape[1], (16, 8, 4, 2, 1))
    attn_s = _attn_sample(q_s, k_s, v_s, lw["sb_bias"], cache_kt, cache_vt, page_table, t_new,
                          pages_per_step).astype(BF16)

    n_tok_p = xp.shape[0]
    tm_post = tm_s
    assert s_len % tm_post == 0
    x1_p, h2_p, idx_p, pr_p, rank_p, counts_p, tail_p = _post(
        xp, attn_p, mods_p[:5], None, jnp.zeros((SUBLANES, LANES), F32), weights,
        prompt=True, n_seq=n_p, t_new=t_new, tm=tm_post, alpha=alpha)
    tpos = jnp.arange(n_rows_s, dtype=jnp.int32) % t_new
    prev = conv_state
    prev_rep = jnp.repeat(prev, t_new, axis=0)
    p1 = jnp.where((tpos == 0)[:, None], prev_rep[:, 1], 0.0)
    p2 = jnp.where((tpos == 0)[:, None], prev_rep[:, 0], jnp.where((tpos == 1)[:, None], prev_rep[:, 1], 0.0))
    x1_s, h2_s, idx_s, pr_s, rank_s, counts_all, cu_s = _post(
        xs, attn_s, mods_s[:5], (p1, p2), counts_p, weights,
        prompt=False, n_seq=n_s, t_new=t_new, tm=tm_s, alpha=alpha)

    idx_flat = jnp.concatenate([idx_p[:, :TOP_K], idx_s[:, :TOP_K]], axis=0).reshape(-1)
    rank_flat = jnp.concatenate([rank_p[:, :TOP_K], rank_s[:, :TOP_K]], axis=0).reshape(-1)
    counts = counts_all[0, :n_exp].astype(jnp.int32)
    plan = _moe_plan(counts, idx_flat.shape[0])
    sorted_rows = (_lookup(plan[0], idx_flat) + rank_flat) * SUBLANES
    xb = _dispatch(h2_p, h2_s, sorted_rows, tm_post)
    yb = _moe(xb, plan, lw["w_gate"], lw["b_gate"], lw["w_up"], lw["b_up"], lw["w_down"], lw["b_down"])

    ln2_g = lw["ln2_g"].reshape(1, d)
    ln2_b = lw["ln2_b"].reshape(1, d)
    out_p = _final(x1_p, pr_p, mods_p[5], yb, sorted_rows, ln2_g, ln2_b,
                   prompt=True, n_seq=n_p, tm=tm_post, alpha=alpha, row_offset=0)
    out_s = _final(x1_s, pr_s, mods_s[5], yb, sorted_rows, ln2_g, ln2_b,
                   prompt=False, n_seq=n_s, tm=tm_s, alpha=alpha, row_offset=n_tok_p)

    n_heads = d_attn // HEAD_DIM
    k_prompt = kt_p.reshape(n_p, n_heads, HEAD_DIM, s_len).transpose(0, 3, 1, 2)
    v_prompt = vt_p.reshape(n_p, n_heads, HEAD_DIM, s_len).transpose(0, 3, 1, 2)
    conv_prompt = tail_p[:, SUBLANES - 2:, :]
    k_sample = k_s.reshape(n_s, t_new, n_heads, HEAD_DIM)
    v_sample = v_s.reshape(n_s, t_new, n_heads, HEAD_DIM)
    conv_sample = cu_s.reshape(n_s, t_new, d_conv)[:, t_new - 2:, :]
    return out_p, out_s, k_prompt, v_prompt, conv_prompt, k_sample, v_sample, conv_sample


def kernel(x_prompt, x_sample, cache_k, cache_v, state_conv, page_table, c_prompt, c_sample, w_ada, b_ada, w_in, sb_bias, w_conv, w_pa, w_pb, w_o, ln1_g, ln1_b, w_router, b_router, w_gate, b_gate, w_up, b_up, w_down, b_down, ln2_g, ln2_b):
    depth = w_in.shape[0]
    n_p, s_len, d = x_prompt.shape
    n_s, t_new, _ = x_sample.shape
    assert t_new >= 2 and state_conv.shape[2] == 2
    n_pool, page, n_heads, head_dim = cache_k.shape[1:]
    assert head_dim == HEAD_DIM
    names = ("w_ada", "b_ada", "w_in", "sb_bias", "w_conv", "w_pa", "w_pb", "w_o", "ln1_g", "ln1_b",
             "w_router", "b_router", "w_gate", "b_gate", "w_up", "b_up", "w_down", "b_down", "ln2_g", "ln2_b")
    stacked = (w_ada, b_ada, w_in, sb_bias, w_conv, w_pa, w_pb, w_o, ln1_g, ln1_b,
               w_router, b_router, w_gate, b_gate, w_up, b_up, w_down, b_down, ln2_g, ln2_b)
    xp = x_prompt.reshape(n_p * s_len, d)
    xs = x_sample.reshape(n_s * t_new, d)
    c_all = jnp.concatenate([c_prompt, c_sample], axis=0)
    outs = [[] for _ in range(6)]
    for l in range(depth):
        lw = {n: a[l] for n, a in zip(names, stacked)}
        cache_kt = cache_k[l].transpose(0, 2, 3, 1).reshape(n_pool, n_heads * head_dim, page)
        cache_vt = cache_v[l].transpose(0, 2, 3, 1).reshape(n_pool, n_heads * head_dim, page)
        xp, xs, *rest = _decoder_layer(xp, xs, c_all, cache_kt, cache_vt, state_conv[l], page_table, lw,
                                       n_p=n_p, n_s=n_s, t_new=t_new, depth=depth)
        for o, r in zip(outs, rest):
            o.append(r)
    stack = [jnp.stack(o) for o in outs]
    return (xp.reshape(n_p, s_len, d), xs.reshape(n_s, t_new, d), *stack)
```

```python
import functools

import jax
import jax.numpy as jnp
from jax import lax
from jax.experimental import pallas as pl
from jax.experimental.pallas import tpu as pltpu

F32 = jnp.float32
BF16 = jnp.bfloat16

HEAD_DIM = 64
TOP_K = 4
MOE_BLOCK = 256
SWIGLU_LIMIT = 7.0
SWIGLU_ALPHA = 1.702
LN_EPS = 1e-5

LANES = 128
SUBLANES = 8
HEADS_PER_GROUP = LANES // HEAD_DIM
VMEM_LIMIT = 56 * 1024 * 1024

_NT = (((1,), (1,)), ((), ()))
MASKED_LOGIT = -1e30


def _params(*sem):
    return pltpu.CompilerParams(dimension_semantics=sem, vmem_limit_bytes=VMEM_LIMIT)


def _layer_norm(x):
    mu = jnp.mean(x, axis=-1, keepdims=True)
    xc = x - mu
    var = jnp.mean(xc * xc, axis=-1, keepdims=True)
    return xc * lax.rsqrt(var + LN_EPS)


def _softplus(z):
    neg_abs = lax.bitcast_convert_type(lax.bitcast_convert_type(z, jnp.int32) | jnp.int32(-2 ** 31), F32)
    return jnp.maximum(z, 0.0) + jnp.log(1.0 + jnp.exp(neg_abs))


def _const_spec(shape):
    return pl.BlockSpec(shape, lambda *_: (0,) * len(shape))


def _ada_kernel(c_ref, w_ref, b_ref, o_ref):
    o_ref[...] = jnp.dot(c_ref[...], w_ref[...], preferred_element_type=F32,
                         precision=lax.Precision.HIGHEST) + b_ref[...]


def _ada(c, w, b):
    m, d = c.shape
    n = w.shape[1]
    tn = n // 6
    return pl.pallas_call(
        _ada_kernel,
        grid=(n // tn,),
        in_specs=[_const_spec((m, d)),
                  pl.BlockSpec((d, tn), lambda j: (0, j)),
                  pl.BlockSpec((1, tn), lambda j: (0, j))],
        out_specs=pl.BlockSpec((m, tn), lambda j: (0, j)),
        out_shape=jax.ShapeDtypeStruct((m, n), F32),
        compiler_params=_params("arbitrary"),
        name="ada",
    )(c, w, b.reshape(1, n))


def _qkv_prompt_kernel(x_ref, shift_ref, scale_ref, wq_ref, wkt_ref, wvt_ref, wv_ref,
                       q_ref, kt_ref, ktb_ref, vt_ref, vb_ref, *, q_scale):
    h = (_layer_norm(x_ref[...]) * (1.0 + scale_ref[0]) + shift_ref[0]).astype(BF16)
    q = jnp.dot(h, wq_ref[...], preferred_element_type=F32)
    q_ref[...] = (q * q_scale).astype(BF16)
    kt = lax.dot_general(wkt_ref[...], h, _NT, preferred_element_type=F32)
    kt_ref[0] = kt
    ktb_ref[0] = kt.astype(BF16)
    vt_ref[0] = lax.dot_general(wvt_ref[...], h, _NT, preferred_element_type=F32)
    vb_ref[...] = jnp.dot(h, wv_ref[...], preferred_element_type=F32).astype(BF16)


def _qkv_prompt(x, shift, scale, wq, wkt, wvt, wv, n_seq, tm):
    n, d = x.shape
    da = wq.shape[1]
    s_len = n // n_seq
    tps = s_len // tm
    mod = pl.BlockSpec((1, 1, d), lambda i: (i // tps, 0, 0))
    row = pl.BlockSpec((tm, da), lambda i: (i, 0))
    tr = pl.BlockSpec((1, da, tm), lambda i: (i // tps, 0, i % tps))
    return pl.pallas_call(
        functools.partial(_qkv_prompt_kernel, q_scale=HEAD_DIM ** -0.5),
        grid=(n // tm,),
        in_specs=[pl.BlockSpec((tm, d), lambda i: (i, 0)), mod, mod,
                  _const_spec((d, da)), _const_spec((da, d)), _const_spec((da, d)), _const_spec((d, da))],
        out_specs=[row, tr, tr, tr, row],
        out_shape=[jax.ShapeDtypeStruct((n, da), BF16),
                   jax.ShapeDtypeStruct((n_seq, da, s_len), F32),
                   jax.ShapeDtypeStruct((n_seq, da, s_len), BF16),
                   jax.ShapeDtypeStruct((n_seq, da, s_len), F32),
                   jax.ShapeDtypeStruct((n, da), BF16)],
        compiler_params=_params("arbitrary"),
        name="qkv_prompt",
    )(x, shift, scale, wq, wkt, wvt, wv)


def _qkv_sample_kernel(x_ref, shift_ref, scale_ref, wq_ref, wk_ref, wv_ref, q_ref, k_ref, v_ref, *, q_scale):
    h = (_layer_norm(x_ref[...]) * (1.0 + scale_ref[0]) + shift_ref[0]).astype(BF16)
    q_ref[...] = (jnp.dot(h, wq_ref[...], preferred_element_type=F32) * q_scale).astype(BF16)
    k_ref[...] = jnp.dot(h, wk_ref[...], preferred_element_type=F32)
    v_ref[...] = jnp.dot(h, wv_ref[...], preferred_element_type=F32)


def _qkv_sample(x, shift, scale, wq, wk, wv, tm):
    n, d = x.shape
    da = wq.shape[1]
    mod = pl.BlockSpec((1, tm, d), lambda i: (i, 0, 0))
    row = pl.BlockSpec((tm, da), lambda i: (i, 0))
    return pl.pallas_call(
        functools.partial(_qkv_sample_kernel, q_scale=HEAD_DIM ** -0.5),
        grid=(n // tm,),
        in_specs=[pl.BlockSpec((tm, d), lambda i: (i, 0)), mod, mod,
                  _const_spec((d, da)), _const_spec((d, da)), _const_spec((d, da))],
        out_specs=[row, row, row],
        out_shape=[jax.ShapeDtypeStruct((n, da), BF16),
                   jax.ShapeDtypeStruct((n, da), F32),
                   jax.ShapeDtypeStruct((n, da), F32)],
        compiler_params=_params("arbitrary"),
        name="qkv_sample",
    )(x, shift, scale, wq, wk, wv)


def _attn_prompt_kernel(bias_ref, q_ref, kt_ref, v_ref, u_ref, o_ref, *, tq, tk):
    g = pl.program_id(1)
    qi = pl.program_id(2)
    q2 = q_ref[...]
    lane = lax.broadcasted_iota(jnp.int32, (tq, LANES), 1)
    n_diag = tq // tk
    row = lax.broadcasted_iota(jnp.int32, (tq, tk), 0)
    col = lax.broadcasted_iota(jnp.int32, (tq, tk), 1)
    u = u_ref[...]
    heads = range(HEADS_PER_GROUP)
    qs = [jnp.where(lane // HEAD_DIM == p, q2, jnp.zeros_like(q2)) for p in heads]
    biases = [bias_ref[g * HEADS_PER_GROUP + p] for p in heads]

    def sub_block(off, state, vis):
        kt = kt_ref[0, :, pl.ds(off, tk)]
        v = v_ref[pl.ds(off, tk), :]
        new_state = []
        for qp, bias, (o, c) in zip(qs, biases, state):
            z = jnp.dot(qp, kt, preferred_element_type=F32) + bias
            if vis is not None:
                z = jnp.where(vis, z, MASKED_LOGIT)
            r = jnp.dot(_softplus(z).astype(BF16), u, preferred_element_type=F32)
            a = jnp.exp(z - r[:, :tk] - jnp.concatenate([c] * (tk // LANES), axis=1))
            o = o + jnp.dot(a.astype(BF16), v, preferred_element_type=F32)
            new_state.append((o, c + r[:, tk:]))
        return tuple(new_state)

    def key_block(kb, state, masked):
        for d in reversed(range(n_diag)):
            off = pl.multiple_of(kb * tq + d * tk, tk)
            state = sub_block(off, state, (d * tk + col < row) if masked else None)
        return state

    zeros = jnp.zeros((tq, LANES), F32)
    state = key_block(qi, tuple((zeros, zeros) for _ in heads), True)
    state = lax.fori_loop(0, qi, lambda j, st: key_block(qi - 1 - j, st, False), state)
    o_ref[...] = jnp.where(lane < HEAD_DIM, state[0][0], state[1][0]).astype(o_ref.dtype)


def _suffix_ones(n, extra=0):
    j = lax.broadcasted_iota(jnp.int32, (n, n + extra), 0)
    s = lax.broadcasted_iota(jnp.int32, (n, n + extra), 1)
    return ((j >= s) | (s >= n)).astype(BF16)


def _attn_prompt(q, ktb, vb, sb_bias, n_seq, tq, tk):
    n, da = q.shape
    s_len = n // n_seq
    nq = s_len // tq
    n_groups = da // LANES
    return pl.pallas_call(
        functools.partial(_attn_prompt_kernel, tq=tq, tk=tk),
        grid_spec=pltpu.PrefetchScalarGridSpec(
            num_scalar_prefetch=1,
            grid=(n_seq, n_groups, nq),
            in_specs=[pl.BlockSpec((tq, LANES), lambda b, g, i, *_: (b * nq + i, g)),
                      pl.BlockSpec((1, LANES, s_len), lambda b, g, i, *_: (b, g, 0)),
                      pl.BlockSpec((s_len, LANES), lambda b, g, i, *_: (b, g)),
                      pl.BlockSpec((tk, tk + LANES), lambda b, g, i, *_: (0, 0))],
            out_specs=pl.BlockSpec((tq, LANES), lambda b, g, i, *_: (b * nq + i, g)),
        ),
        out_shape=jax.ShapeDtypeStruct((n, da), BF16),
        compiler_params=_params("arbitrary", "arbitrary", "arbitrary"),
        name="attn_prompt",
    )(sb_bias, q, ktb, vb, _suffix_ones(tk, LANES))


def _attn_sample_kernel(pt_ref, qbd_ref, bias_ref, knew_ref, vnew_ref, u_ref, ck_hbm, cv_hbm,
                        o_ref, acc_ref, carry_ref, kbuf, vbuf, kpages, vpages, sems,
                        *, pages_per_step, n_pages, n_heads, t_new):
    b = pl.program_id(0)
    j = pl.program_id(1)
    n_steps = pl.num_programs(1)
    step = b * n_steps + j
    slot = step % 2
    n_rows = t_new * n_heads
    page = kbuf.shape[1]
    qbd = qbd_ref[0]
    bias = bias_ref[...]
    u = u_ref[...]

    def page_copies(seq, jj, s):
        copies = []
        for i in range(pages_per_step):
            pid = pt_ref[seq * n_pages + (n_pages - 1 - (jj * pages_per_step + i))]
            copies.append(pltpu.make_async_copy(ck_hbm.at[pid], kpages.at[s, i], sems.at[0, s]))
            copies.append(pltpu.make_async_copy(cv_hbm.at[pid], vpages.at[s, i], sems.at[1, s]))
        return copies

    @pl.when(step == 0)
    def _():
        for cp in page_copies(0, 0, 0):
            cp.start()

    @pl.when(step + 1 < pl.num_programs(0) * n_steps)
    def _():
        wrap = j + 1 == n_steps
        for cp in page_copies(jnp.where(wrap, b + 1, b), jnp.where(wrap, 0, j + 1), 1 - slot):
            cp.start()

    def blocks(kts, vts, vis, o, c):
        zs = [jnp.dot(qbd, kt.astype(BF16), preferred_element_type=F32) + bias for kt in kts]
        sps = [_softplus(z) for z in zs]
        if vis is not None:
            sps = [jnp.where(vis, sp, 0.0) for sp in sps]
        rs = [jnp.dot(sp.astype(BF16), u, preferred_element_type=F32) for sp in sps]
        for z, r, vt in zip(zs, rs, vts):
            a = jnp.exp(z - r[:, :page] - jnp.concatenate([c] * (page // LANES), axis=1))
            if vis is not None:
                a = jnp.where(vis, a, 0.0)
            o = o + lax.dot_general(a.astype(BF16), vt.astype(BF16), _NT, preferred_element_type=F32)
            c = c + r[:, page:]
        return o, c

    @pl.when(j == 0)
    def _():
        kbuf[...] = jnp.zeros(kbuf.shape, F32)
        vbuf[...] = jnp.zeros(vbuf.shape, F32)
        kbuf[:, 0:t_new] = knew_ref[0]
        vbuf[:, 0:t_new] = vnew_ref[0]
        row_t = lax.broadcasted_iota(jnp.int32, (n_rows, page), 0) // n_heads
        col = lax.broadcasted_iota(jnp.int32, (n_rows, page), 1)
        o, c = blocks([kbuf[...]], [vbuf[...]], col < row_t,
                      jnp.zeros(acc_ref.shape, F32), jnp.zeros(carry_ref.shape, F32))
        acc_ref[...] = o
        carry_ref[...] = c

    for cp in page_copies(b, j, slot):
        cp.wait()
    o, c = blocks([kpages[slot, i] for i in range(pages_per_step)],
                  [vpages[slot, i] for i in range(pages_per_step)], None, acc_ref[...], carry_ref[...])
    acc_ref[...] = o
    carry_ref[...] = c

    @pl.when(j == pl.num_programs(1) - 1)
    def _():
        r = lax.broadcasted_iota(jnp.int32, o.shape, 0)
        l = lax.broadcasted_iota(jnp.int32, o.shape, 1)
        own = jnp.where(l // HEAD_DIM == r % n_heads, o, 0.0)
        o_ref[0] = own.reshape(t_new, n_heads, o.shape[1]).sum(axis=1)


def _attn_sample(q, k_new, v_new, sb_bias, cache_kt, cache_vt, page_table, t_new, pages_per_step):
    n, da = q.shape
    n_seq = n // t_new
    n_heads = da // HEAD_DIM
    n_pages = page_table.shape[1]
    page = cache_kt.shape[2]
    n_rows = t_new * n_heads
    assert n_pages % pages_per_step == 0 and t_new <= page
    head_of_col = jnp.arange(da, dtype=jnp.int32) // HEAD_DIM
    blockdiag = head_of_col[None, :] == jnp.arange(n_heads, dtype=jnp.int32)[:, None]
    qbd = jnp.where(blockdiag[None, None], q.reshape(n_seq, t_new, 1, da), jnp.zeros((), q.dtype))
    qbd = qbd.reshape(n_seq, n_rows, da)
    assert page % LANES == 0
    bias_rows = jnp.broadcast_to(jnp.tile(sb_bias.astype(F32), t_new).reshape(n_rows, 1), (n_rows, page))
    knew_t = k_new.reshape(n_seq, t_new, da).transpose(0, 2, 1)
    vnew_t = v_new.reshape(n_seq, t_new, da).transpose(0, 2, 1)

    seq3 = lambda b, j, pt: (b, 0, 0)
    const2 = lambda b, j, pt: (0, 0)
    specs = [pl.BlockSpec((1, n_rows, da), seq3),
             pl.BlockSpec((n_rows, page), const2),
             pl.BlockSpec((1, da, t_new), seq3),
             pl.BlockSpec((1, da, t_new), seq3),
             pl.BlockSpec((page, page + LANES), const2),
             pl.BlockSpec(memory_space=pl.ANY),
             pl.BlockSpec(memory_space=pl.ANY)]
    page_buf = pltpu.VMEM((2, pages_per_step, da, page), F32)
    out = pl.pallas_call(
        functools.partial(_attn_sample_kernel, pages_per_step=pages_per_step, n_pages=n_pages,
                          n_heads=n_heads, t_new=t_new),
        grid_spec=pltpu.PrefetchScalarGridSpec(
            num_scalar_prefetch=1,
            grid=(n_seq, n_pages // pages_per_step),
            in_specs=specs,
            out_specs=pl.BlockSpec((1, t_new, da), seq3),
            scratch_shapes=[pltpu.VMEM((n_rows, da), F32), pltpu.VMEM((n_rows, LANES), F32),
                            pltpu.VMEM((da, page), F32), pltpu.VMEM((da, page), F32),
                            page_buf, page_buf, pltpu.SemaphoreType.DMA((2, 2))],
        ),
        out_shape=jax.ShapeDtypeStruct((n_seq, t_new, da), F32),
        compiler_params=_params("arbitrary", "arbitrary"),
        name="attn_sample",
    )(page_table.reshape(-1), qbd, bias_rows, knew_t, vnew_t, _suffix_ones(page, LANES), cache_kt, cache_vt)
    return out.reshape(n, da)


def _post_kernel(i, *refs, prompt, n_sub, tiles_per_seq, t_new, alpha, d_conv, n_experts):
    if prompt:
        (x_ref, attn_ref, sh1, sc1, g1, sh2, sc2, cnt0_ref, lt_ref, wr_ref, wc_ref, wpa_ref, wpb_ref, wo_ref,
         ln1g_ref, ln1b_ref, wrt_ref, br_ref,
         x1_ref, h2_ref, idx_ref, p_ref, rank_ref, cnt_out_ref, cu_ref, cnt_ref, tail_ref) = refs
    else:
        (x_ref, attn_ref, sh1, sc1, g1, sh2, sc2, cnt0_ref, lt_ref, p1_ref, p2_ref,
         wr_ref, wc_ref, wpa_ref, wpb_ref, wo_ref, ln1g_ref, ln1b_ref, wrt_ref, br_ref,
         x1_ref, h2_ref, idx_ref, p_ref, rank_ref, cnt_out_ref, cu_ref, cnt_ref) = refs
    tm, d = x_ref.shape
    hm = tm // n_sub
    dc = d_conv
    subs = range(n_sub)
    rows = [pl.ds(h * hm, hm) for h in subs]

    def mod(m_ref):
        return [m_ref[0] for _ in subs] if prompt else [m_ref[0, r, :] for r in rows]

    @pl.when(i == 0)
    def _():
        cnt_ref[...] = cnt0_ref[...]

    xs = [x_ref[r, :] for r in rows]
    hs = [(_layer_norm(x) * (1.0 + sc) + sh).astype(BF16) for x, sc, sh in zip(xs, mod(sc1), mod(sh1))]
    ys = [jnp.dot(h, wr_ref[...], preferred_element_type=F32) for h in hs]
    cus = [y[:, 2 * dc:3 * dc] * y[:, 0:dc] for y in ys]
    wc = wc_ref[...]
    convs = []
    for h in subs:
        cu = cus[h]
        r1 = pltpu.roll(cu, 1, 0)
        r2 = pltpu.roll(cu, 2, 0)
        if prompt:
            if h == 0:
                tail = jnp.where((i % tiles_per_seq) == 0, 0.0, tail_ref[...])
            else:
                tail = cus[h - 1][hm - SUBLANES:, :]
            row8 = lax.broadcasted_iota(jnp.int32, (SUBLANES, dc), 0)
            f1 = jnp.where(row8 >= 1, r1[0:SUBLANES], pltpu.roll(tail, 1, 0))
            f2 = jnp.where(row8 >= 2, r2[0:SUBLANES], pltpu.roll(tail, 2, 0))
            z1 = jnp.concatenate([f1, r1[SUBLANES:]], axis=0)
            z2 = jnp.concatenate([f2, r2[SUBLANES:]], axis=0)
        else:
            tpos = lax.broadcasted_iota(jnp.int32, (hm, dc), 0) % t_new
            z1 = jnp.where(tpos >= 1, r1, p1_ref[rows[h], :])
            z2 = jnp.where(tpos >= 2, r2, p2_ref[rows[h], :])
            cu_ref[rows[h], :] = cu
        convs.append(wc[0:1] * z2 + wc[1:2] * z1 + wc[2:3] * cu)
    if prompt:
        tail_ref[...] = cus[-1][hm - SUBLANES:, :]
        cu_ref[0] = cus[-1][hm - SUBLANES:, :]
    branch_bs = [jnp.dot((y[:, dc:2 * dc] * conv).astype(BF16), wpb_ref[...], preferred_element_type=F32)
                 for y, conv in zip(ys, convs)]
    branch_as = [jnp.dot(attn_ref[r, :], wpa_ref[...], preferred_element_type=F32) for r in rows]
    mergeds = [jax.nn.sigmoid(y[:, 3 * dc:3 * dc + d]) * ba + jax.nn.sigmoid(y[:, 3 * dc + d:]) * bb
               for y, ba, bb in zip(ys, branch_as, branch_bs)]
    yos = [jnp.dot(m.astype(BF16), wo_ref[...], preferred_element_type=F32) for m in mergeds]
    x1s = [_layer_norm(alpha * x + (1.0 + g) * yo) * ln1g_ref[...] + ln1b_ref[...]
           for x, g, yo in zip(xs, mod(g1), yos)]
    h2s = [_layer_norm(x1) * (1.0 + sc) + sh for x1, sc, sh in zip(x1s, mod(sc2), mod(sh2))]
    logits = [lax.dot_general(h2, wrt_ref[...], _NT, preferred_element_type=F32,
                              precision=lax.Precision.HIGHEST) + br_ref[...] for h2 in h2s]
    for h in subs:
        x1_ref[rows[h], :] = x1s[h]
        _rows_to_tiles(h2_ref.at[pl.ds(h * hm * SUBLANES, hm * SUBLANES)], h2s[h])

    lane_e = lax.broadcasted_iota(jnp.int32, (hm, n_experts), 1)
    lane_o = lax.broadcasted_iota(jnp.int32, (hm, LANES), 1)
    idx_outs = [jnp.zeros((hm, LANES), jnp.int32) for _ in subs]
    e_outs = [jnp.zeros((hm, LANES), F32) for _ in subs]
    denoms = [jnp.zeros((hm, 1), F32) for _ in subs]
    tops = [None for _ in subs]
    sels = [[] for _ in subs]
    for k in range(TOP_K):
        for h in subs:
            m = jnp.max(logits[h], axis=-1, keepdims=True)
            sel = jnp.min(jnp.where(logits[h] == m, lane_e, n_experts), axis=-1, keepdims=True)
            logits[h] = jnp.where(lane_e == sel, -jnp.inf, logits[h])
            if k == 0:
                tops[h] = m
            e = jnp.exp(m - tops[h])
            denoms[h] = denoms[h] + e
            idx_outs[h] = jnp.where(lane_o == k, sel, idx_outs[h])
            e_outs[h] = jnp.where(lane_o == k, e, e_outs[h])
            sels[h].append(sel)
    for h in subs:
        idx_ref[rows[h], :] = idx_outs[h]
        p_ref[rows[h], :] = e_outs[h] / denoms[h]

    base = cnt_ref[0:1, :]
    for h in subs:
        onehot = jnp.zeros((hm, LANES), F32)
        for sel in sels[h]:
            onehot = onehot + (lane_o == sel).astype(F32)
        before = jnp.dot(lt_ref[...], onehot.astype(BF16), preferred_element_type=F32) + base
        rank_out = jnp.zeros((hm, LANES), jnp.int32)
        for k, sel in enumerate(sels[h]):
            rk = jnp.sum(jnp.where(lane_o == sel, before, 0.0), axis=-1, keepdims=True)
            rank_out = jnp.where(lane_o == k, rk.astype(jnp.int32), rank_out)
        rank_ref[rows[h], :] = rank_out
        base = base + jnp.sum(onehot, axis=0, keepdims=True)
    cnt = jnp.broadcast_to(base, cnt_ref.shape)
    cnt_ref[...] = cnt
    cnt_out_ref[...] = cnt


def _post(x, attn, mods, conv_hist, counts0, w, *, prompt, n_seq, t_new, tm, n_sub, alpha):
    n, d = x.shape
    da = attn.shape[1]
    dc = w["w_conv"].shape[1]
    n_exp = w["w_router_t"].shape[0]
    assert n_exp <= LANES and tm % n_sub == 0
    hm = tm // n_sub
    tiles_per_seq = (n // n_seq) // tm if prompt else 0
    if prompt:
        mod = pl.BlockSpec((1, 1, d), lambda i: (i // tiles_per_seq, 0, 0))
    else:
        mod = pl.BlockSpec((1, tm, d), lambda i: (i, 0, 0))
    rows = lambda c: pl.BlockSpec((tm, c), lambda i: (i, 0))
    t_row = lax.broadcasted_iota(jnp.int32, (hm, hm), 0)
    t_col = lax.broadcasted_iota(jnp.int32, (hm, hm), 1)
    strictly_lower = (t_col < t_row).astype(BF16)
    in_specs = [rows(d), rows(da)] + [mod] * 5 + [_const_spec((SUBLANES, LANES)), _const_spec((hm, hm))]
    args = [x, attn, *mods, counts0, strictly_lower]
    if not prompt:
        in_specs += [rows(dc), rows(dc)]
        args += list(conv_hist)
    weights = [w["w_rest"], w["w_conv"], w["w_pa"], w["w_pb"], w["w_o"], w["ln1_g"], w["ln1_b"],
               w["w_router_t"], w["b_router"]]
    in_specs += [_const_spec(a.shape) for a in weights]
    args += weights
    assert d == SUBLANES * LANES
    out_specs = [rows(d), pl.BlockSpec((tm * SUBLANES, LANES), lambda i: (i, 0)),
                 rows(LANES), rows(LANES), rows(LANES), _const_spec((SUBLANES, LANES))]
    out_shape = [jax.ShapeDtypeStruct((n, d), F32),
                 jax.ShapeDtypeStruct((n * SUBLANES, LANES), F32),
                 jax.ShapeDtypeStruct((n, LANES), jnp.int32),
                 jax.ShapeDtypeStruct((n, LANES), F32),
                 jax.ShapeDtypeStruct((n, LANES), jnp.int32),
                 jax.ShapeDtypeStruct((SUBLANES, LANES), F32)]
    scratch = [pltpu.VMEM((SUBLANES, LANES), F32)]
    if prompt:
        out_specs.append(pl.BlockSpec((1, SUBLANES, dc), lambda i: (i // tiles_per_seq, 0, 0)))
        out_shape.append(jax.ShapeDtypeStruct((n_seq, SUBLANES, dc), F32))
        scratch.append(pltpu.VMEM((SUBLANES, dc), F32))
    else:
        out_specs.append(rows(dc))
        out_shape.append(jax.ShapeDtypeStruct((n, dc), F32))

    def body(*refs):
        _post_kernel(pl.program_id(0), *refs, prompt=prompt, n_sub=n_sub, tiles_per_seq=tiles_per_seq,
                     t_new=t_new, alpha=alpha, d_conv=dc, n_experts=n_exp)

    return pl.pallas_call(
        body,
        grid=(n // tm,),
        in_specs=in_specs,
        out_specs=out_specs,
        out_shape=out_shape,
        scratch_shapes=scratch,
        compiler_params=_params("arbitrary"),
        name="post_prompt" if prompt else "post_sample",
    )(*args)


def _count_le(sorted_vals, x):
    return jnp.sum((sorted_vals[None, :] <= x[:, None]).astype(jnp.int32), axis=1)


def _lookup(table, idx):
    hit = idx[:, None] == jnp.arange(table.shape[0], dtype=jnp.int32)[None, :]
    return jnp.sum(jnp.where(hit, table[None, :], 0), axis=1)


def _moe_plan(counts, n_assign):
    n_exp = counts.shape[0]
    assert n_assign % MOE_BLOCK == 0
    n_tiles = n_assign // MOE_BLOCK
    end = jnp.cumsum(counts)
    start = end - counts
    tile_lo = jnp.arange(n_tiles, dtype=jnp.int32) * MOE_BLOCK
    e_lo = jnp.minimum(_count_le(end, tile_lo), n_exp - 1)
    e_hi = jnp.minimum(_count_le(end, tile_lo + (MOE_BLOCK - 1)), n_exp - 1)
    n_vis = e_hi - e_lo + 1
    v_end = jnp.cumsum(n_vis)
    v_start = v_end - n_vis
    n_visits = n_tiles + n_exp - 1
    v = jnp.arange(n_visits, dtype=jnp.int32)
    used = v < v_end[-1]
    tile_v = jnp.minimum(_count_le(v_end, v), n_tiles - 1)
    first_v = used & (v == _lookup(v_start, tile_v))
    expert_v = jnp.where(used, _lookup(e_lo, tile_v) + v - _lookup(v_start, tile_v), e_hi[-1])
    base = tile_v * MOE_BLOCK
    lo_v = jnp.where(used, jnp.clip(_lookup(start, expert_v) - base, 0, MOE_BLOCK), 0)
    hi_v = jnp.where(used, jnp.clip(_lookup(end, expert_v) - base, 0, MOE_BLOCK), 0)
    as_i32 = lambda a: a.astype(jnp.int32)
    return as_i32(start), as_i32(tile_v), as_i32(expert_v), as_i32(lo_v), as_i32(hi_v), as_i32(first_v)


def _rows_to_tiles(ref, x):
    m, d = x.shape
    per = d // LANES
    for s in range(per):
        ref[pl.ds(s, m, stride=per), :] = x[:, s * LANES:(s + 1) * LANES]


def _tiles_to_rows(ref, m, lead=()):
    per = ref.shape[-2] // m
    return jnp.concatenate([ref[(*lead, pl.ds(s, m, stride=per), slice(None))] for s in range(per)], axis=1)


def _for_assignments(tm, fn):
    def row(r, carry):
        for k in range(TOP_K):
            fn(r, k, r * TOP_K + k)
        return carry
    lax.fori_loop(0, tm, row, 0, unroll=2)


def _tile_of(ref, t, lead=()):
    return ref.at[(*lead, pl.ds(pl.multiple_of(t, SUBLANES), SUBLANES))]


def _dispatch_kernel(dst_ref, hp_ref, hs_ref, xb_hbm, sem, *, n_tiles_p, tm):
    i = pl.program_id(0)

    def send_tile(h_ref):
        def copy(r, k, a):
            return pltpu.make_async_copy(_tile_of(h_ref, r * SUBLANES), _tile_of(xb_hbm, dst_ref[a]), sem)
        _for_assignments(tm, lambda r, k, a: copy(r, k, a).start(priority=k % 2))
        _for_assignments(tm, lambda r, k, a: copy(r, k, a).wait())

    @pl.when(i < n_tiles_p)
    def _():
        send_tile(hp_ref)

    @pl.when(i >= n_tiles_p)
    def _():
        send_tile(hs_ref)


def _dispatch(h2_p, h2_s, dst_rows, tm):
    per = SUBLANES
    n_tiles_p = h2_p.shape[0] // (tm * per)
    n_tiles_s = h2_s.shape[0] // (tm * per)
    flat = pl.BlockSpec((tm * TOP_K,), lambda i: (i,), memory_space=pltpu.SMEM)
    return pl.pallas_call(
        functools.partial(_dispatch_kernel, n_tiles_p=n_tiles_p, tm=tm),
        grid=(n_tiles_p + n_tiles_s,),
        in_specs=[flat,
                  pl.BlockSpec((tm * per, LANES), lambda i: (jnp.minimum(i, n_tiles_p - 1), 0)),
                  pl.BlockSpec((tm * per, LANES), lambda i: (jnp.maximum(i - n_tiles_p, 0), 0))],
        out_specs=pl.BlockSpec(memory_space=pl.ANY),
        scratch_shapes=[pltpu.SemaphoreType.DMA(())],
        out_shape=jax.ShapeDtypeStruct((dst_rows.shape[0] * per, LANES), F32),
        compiler_params=_params("arbitrary"),
        name="dispatch",
    )(dst_rows, h2_p, h2_s)


def _moe_kernel(tile_ref, exp_ref, lo_ref, hi_ref, first_ref, x_ref, wg_ref, bg_ref, wu_ref, bu_ref,
                wd_ref, bd_ref, o_ref, wg_bf, wu_bf, wd_bf):
    v = pl.program_id(0)
    lo = lo_ref[v]
    hi = hi_ref[v]

    @pl.when((v == 0) | (exp_ref[v] != exp_ref[jnp.maximum(v - 1, 0)]))
    def _():
        wg_bf[...] = wg_ref[0].astype(BF16)
        wu_bf[...] = wu_ref[0].astype(BF16)
        wd_bf[...] = wd_ref[0].astype(BF16)

    @pl.when(hi > lo)
    def _():
        x = _tiles_to_rows(x_ref, MOE_BLOCK).astype(BF16)
        g = jnp.minimum(jnp.dot(x, wg_bf[...], preferred_element_type=F32) + bg_ref[0], SWIGLU_LIMIT)
        u = jnp.clip(jnp.dot(x, wu_bf[...], preferred_element_type=F32) + bu_ref[0], -SWIGLU_LIMIT, SWIGLU_LIMIT)
        act = g * jax.nn.sigmoid(SWIGLU_ALPHA * g) * (u + 1.0)
        y = jnp.dot(act.astype(BF16), wd_bf[...], preferred_element_type=F32) + bd_ref[0]
        row = lax.broadcasted_iota(jnp.int32, y.shape, 0)
        y = jnp.where((row >= lo) & (row < hi), y, 0.0)

        @pl.when(first_ref[v] == 1)
        def _():
            _rows_to_tiles(o_ref, y)

        @pl.when(first_ref[v] == 0)
        def _():
            _rows_to_tiles(o_ref, y + _tiles_to_rows(o_ref, MOE_BLOCK))


def _moe(xb, plan, wg, bg, wu, bu, wd, bd):
    _, tile_v, expert_v, lo_v, hi_v, first_v = plan
    n_exp, d, de = wg.shape
    assert d == SUBLANES * LANES
    tile = pl.BlockSpec((MOE_BLOCK * SUBLANES, LANES), lambda v, t, e, *_: (t[v], 0))
    wspec = lambda a, b: pl.BlockSpec((1, a, b), lambda v, t, e, *_: (e[v], 0, 0))
    return pl.pallas_call(
        _moe_kernel,
        grid_spec=pltpu.PrefetchScalarGridSpec(
            num_scalar_prefetch=5,
            grid=(tile_v.shape[0],),
            in_specs=[tile, wspec(d, de), wspec(1, de), wspec(d, de), wspec(1, de), wspec(de, d), wspec(1, d)],
            out_specs=tile,
            scratch_shapes=[pltpu.VMEM((d, de), BF16), pltpu.VMEM((d, de), BF16), pltpu.VMEM((de, d), BF16)],
        ),
        out_shape=jax.ShapeDtypeStruct(xb.shape, F32),
        compiler_params=_params("arbitrary"),
        name="moe",
    )(tile_v, expert_v, lo_v, hi_v, first_v, xb, wg, bg.reshape(n_exp, 1, de), wu, bu.reshape(n_exp, 1, de),
      wd, bd.reshape(n_exp, 1, d))


def _final_kernel(src_ref, src_next_ref, x1_ref, p_ref, g2_ref, lng_ref, lnb_ref, yb_hbm, o_ref, ybuf, sems,
                  *, alpha, tm):
    i = pl.program_id(0)
    slot = i % 2

    def copy(src, s, r, k):
        return pltpu.make_async_copy(_tile_of(yb_hbm, src), _tile_of(ybuf, r * SUBLANES, (s, k)), sems.at[s])

    def fetch(src_r, s):
        _for_assignments(tm, lambda r, k, a: copy(src_r[a], s, r, k).start(priority=k % 2))

    @pl.when(i == 0)
    def _():
        fetch(src_ref, 0)

    @pl.when(i + 1 < pl.num_programs(0))
    def _():
        fetch(src_next_ref, 1 - slot)

    _for_assignments(tm, lambda r, k, a: copy(0, slot, r, k).wait())
    p = p_ref[...]
    ys = [_tiles_to_rows(ybuf, tm, (slot, k)) for k in range(TOP_K)]
    y = (p[:, 0:1] * ys[0] + p[:, 1:2] * ys[1]) + (p[:, 2:3] * ys[2] + p[:, 3:4] * ys[3])
    o_ref[...] = _layer_norm(alpha * x1_ref[...] + (1.0 + g2_ref[0]) * y) * lng_ref[...] + lnb_ref[...]


def _final(x1, probs, gate2, yb, src_rows, ln_g, ln_b, *, prompt, n_seq, tm, alpha, row_offset):
    n, d = x1.shape
    n_tiles = n // tm
    assert row_offset % tm == 0
    off = row_offset // tm
    if prompt:
        tiles_per_seq = (n // n_seq) // tm
        mod = pl.BlockSpec((1, 1, d), lambda i: (i // tiles_per_seq, 0, 0))
    else:
        mod = pl.BlockSpec((1, tm, d), lambda i: (i, 0, 0))
    rows = lambda c: pl.BlockSpec((tm, c), lambda i: (i, 0))
    flat = pl.BlockSpec((tm * TOP_K,), lambda i: (off + i,), memory_space=pltpu.SMEM)
    flat_next = pl.BlockSpec((tm * TOP_K,), lambda i: (off + jnp.minimum(i + 1, n_tiles - 1),),
                             memory_space=pltpu.SMEM)
    return pl.pallas_call(
        functools.partial(_final_kernel, alpha=alpha, tm=tm),
        grid=(n_tiles,),
        in_specs=[flat, flat_next, rows(d), rows(LANES), mod, _const_spec((1, d)), _const_spec((1, d)),
                  pl.BlockSpec(memory_space=pl.ANY)],
        out_specs=rows(d),
        scratch_shapes=[pltpu.VMEM((2, TOP_K, tm * SUBLANES, LANES), F32), pltpu.SemaphoreType.DMA((2,))],
        out_shape=jax.ShapeDtypeStruct((n, d), F32),
        compiler_params=_params("arbitrary"),
        name="final_prompt" if prompt else "final_sample",
    )(src_rows, src_rows, x1, probs, gate2, ln_g, ln_b, yb)


def _pick(n, candidates):
    for c in candidates:
        if n % c == 0:
            return c
    raise ValueError(f"no tile size in {candidates} divides {n}")


def _decoder_layer(xp, xs, c_all, cache_kt, cache_vt, conv_state, page_table, lw, *, n_p, n_s, t_new, depth):
    d = xp.shape[1]
    s_len = xp.shape[0] // n_p
    n_rows_s = xs.shape[0]
    alpha = (2 * depth) ** 0.25
    d_attn = lw["w_pa"].shape[0]
    d_conv = lw["w_conv"].shape[1]
    n_exp = lw["w_router"].shape[1]

    ada = _ada(c_all, lw["w_ada"], lw["b_ada"])
    mods = [ada[:, k * d:(k + 1) * d] for k in range(6)]
    tm_s = _pick(n_rows_s, (256, 128, 64, 32, 16, 8))
    mods_p = [m[:n_p].reshape(n_p, 1, d) for m in mods]
    mods_s = [jnp.repeat(m[n_p:], t_new, axis=0).reshape(n_rows_s // tm_s, tm_s, d) for m in mods]

    w_in = lw["w_in"]
    wq = w_in[:, :d_attn].astype(BF16)
    wk = w_in[:, d_attn:2 * d_attn].astype(BF16)
    wv = w_in[:, 2 * d_attn:3 * d_attn].astype(BF16)
    weights = {
        "w_rest": w_in[:, 3 * d_attn:].astype(BF16),
        "w_conv": lw["w_conv"],
        "w_pa": lw["w_pa"].astype(BF16), "w_pb": lw["w_pb"].astype(BF16), "w_o": lw["w_o"].astype(BF16),
        "ln1_g": lw["ln1_g"].reshape(1, d), "ln1_b": lw["ln1_b"].reshape(1, d),
        "w_router_t": lw["w_router"].T, "b_router": lw["b_router"].reshape(1, n_exp),
    }

    tm_p = _pick(s_len, (512, 256, 128))
    q_p, kt_p, ktb_p, vt_p, vb_p = _qkv_prompt(xp, mods_p[0], mods_p[1], wq, wk.T, wv.T, wv, n_p, tm_p)
    tq = _pick(s_len, (512, 256))
    attn_p = _attn_prompt(q_p, ktb_p, vb_p, lw["sb_bias"], n_p, tq, 256)

    q_s, k_s, v_s = _qkv_sample(xs, mods_s[0], mods_s[1], wq, wk, wv, tm_s)
    pages_per_step = _pick(page_table.shape[1], (16, 8, 4, 2, 1))
    attn_s = _attn_sample(q_s, k_s, v_s, lw["sb_bias"], cache_kt, cache_vt, page_table, t_new,
                          pages_per_step).astype(BF16)

    n_tok_p = xp.shape[0]
    tm_post = tm_s
    assert s_len % (2 * tm_post) == 0
    x1_p, h2_p, idx_p, pr_p, rank_p, counts_p, tail_p = _post(
        xp, attn_p, mods_p[:5], None, jnp.zeros((SUBLANES, LANES), F32), weights,
        prompt=True, n_seq=n_p, t_new=t_new, tm=2 * tm_post, n_sub=2, alpha=alpha)
    tpos = jnp.arange(n_rows_s, dtype=jnp.int32) % t_new
    prev = conv_state
    prev_rep = jnp.repeat(prev, t_new, axis=0)
    p1 = jnp.where((tpos == 0)[:, None], prev_rep[:, 1], 0.0)
    p2 = jnp.where((tpos == 0)[:, None], prev_rep[:, 0], jnp.where((tpos == 1)[:, None], prev_rep[:, 1], 0.0))
    x1_s, h2_s, idx_s, pr_s, rank_s, counts_all, cu_s = _post(
        xs, attn_s, mods_s[:5], (p1, p2), counts_p, weights,
        prompt=False, n_seq=n_s, t_new=t_new, tm=tm_s, n_sub=1, alpha=alpha)

    idx_flat = jnp.concatenate([idx_p[:, :TOP_K], idx_s[:, :TOP_K]], axis=0).reshape(-1)
    rank_flat = jnp.concatenate([rank_p[:, :TOP_K], rank_s[:, :TOP_K]], axis=0).reshape(-1)
    counts = counts_all[0, :n_exp].astype(jnp.int32)
    plan = _moe_plan(counts, idx_flat.shape[0])
    sorted_rows = (_lookup(plan[0], idx_flat) + rank_flat) * SUBLANES
    xb = _dispatch(h2_p, h2_s, sorted_rows, tm_post)
    yb = _moe(xb, plan, lw["w_gate"], lw["b_gate"], lw["w_up"], lw["b_up"], lw["w_down"], lw["b_down"])

    ln2_g = lw["ln2_g"].reshape(1, d)
    ln2_b = lw["ln2_b"].reshape(1, d)
    out_p = _final(x1_p, pr_p, mods_p[5], yb, sorted_rows, ln2_g, ln2_b,
                   prompt=True, n_seq=n_p, tm=tm_post, alpha=alpha, row_offset=0)
    out_s = _final(x1_s, pr_s, mods_s[5], yb, sorted_rows, ln2_g, ln2_b,
                   prompt=False, n_seq=n_s, tm=tm_s, alpha=alpha, row_offset=n_tok_p)

    n_heads = d_attn // HEAD_DIM
    k_prompt = kt_p.reshape(n_p, n_heads, HEAD_DIM, s_len).transpose(0, 3, 1, 2)
    v_prompt = vt_p.reshape(n_p, n_heads, HEAD_DIM, s_len).transpose(0, 3, 1, 2)
    conv_prompt = tail_p[:, SUBLANES - 2:, :]
    k_sample = k_s.reshape(n_s, t_new, n_heads, HEAD_DIM)
    v_sample = v_s.reshape(n_s, t_new, n_heads, HEAD_DIM)
    conv_sample = cu_s.reshape(n_s, t_new, d_conv)[:, t_new - 2:, :]
    return out_p, out_s, k_prompt, v_prompt, conv_prompt, k_sample, v_sample, conv_sample


def kernel(x_prompt, x_sample, cache_k, cache_v, state_conv, page_table, c_prompt, c_sample, w_ada, b_ada, w_in, sb_bias, w_conv, w_pa, w_pb, w_o, ln1_g, ln1_b, w_router, b_router, w_gate, b_gate, w_up, b_up, w_down, b_down, ln2_g, ln2_b):
    depth = w_in.shape[0]
    n_p, s_len, d = x_prompt.shape
    n_s, t_new, _ = x_sample.shape
    assert t_new >= 2 and state_conv.shape[2] == 2
    n_pool, page, n_heads, head_dim = cache_k.shape[1:]
    assert head_dim == HEAD_DIM
    names = ("w_ada", "b_ada", "w_in", "sb_bias", "w_conv", "w_pa", "w_pb", "w_o", "ln1_g", "ln1_b",
             "w_router", "b_router", "w_gate", "b_gate", "w_up", "b_up", "w_down", "b_down", "ln2_g", "ln2_b")
    stacked = (w_ada, b_ada, w_in, sb_bias, w_conv, w_pa, w_pb, w_o, ln1_g, ln1_b,
               w_router, b_router, w_gate, b_gate, w_up, b_up, w_down, b_down, ln2_g, ln2_b)
    xp = x_prompt.reshape(n_p * s_len, d)
    xs = x_sample.reshape(n_s * t_new, d)
    c_all = jnp.concatenate([c_prompt, c_sample], axis=0)
    outs = [[] for _ in range(6)]
    for l in range(depth):
        lw = {n: a[l] for n, a in zip(names, stacked)}
        cache_kt = cache_k[l].transpose(0, 2, 3, 1).reshape(n_pool, n_heads * head_dim, page)
        cache_vt = cache_v[l].transpose(0, 2, 3, 1).reshape(n_pool, n_heads * head_dim, page)
        xp, xs, *rest = _decoder_layer(xp, xs, c_all, cache_kt, cache_vt, state_conv[l], page_table, lw,
                                       n_p=n_p, n_s=n_s, t_new=t_new, depth=depth)
        for o, r in zip(outs, rest):
            o.append(r)
    stack = [jnp.stack(o) for o in outs]
    return (xp.reshape(n_p, s_len, d), xs.reshape(n_s, t_new, d), *stack)
```

```python
import functools

import jax
import jax.numpy as jnp
from jax import lax
from jax.experimental import pallas as pl
from jax.experimental.pallas import tpu as pltpu

F32 = jnp.float32
BF16 = jnp.bfloat16

HEAD_DIM = 64
TOP_K = 4
MOE_BLOCK = 256
SWIGLU_LIMIT = 7.0
SWIGLU_ALPHA = 1.702
LN_EPS = 1e-5

LANES = 128
SUBLANES = 8
HEADS_PER_GROUP = LANES // HEAD_DIM
VMEM_LIMIT = 56 * 1024 * 1024

_NT = (((1,), (1,)), ((), ()))
MASKED_LOGIT = -1e30


def _params(*sem):
    return pltpu.CompilerParams(dimension_semantics=sem, vmem_limit_bytes=VMEM_LIMIT)


def _layer_norm(x):
    mu = jnp.mean(x, axis=-1, keepdims=True)
    xc = x - mu
    var = jnp.mean(xc * xc, axis=-1, keepdims=True)
    return xc * lax.rsqrt(var + LN_EPS)


def _softplus(z):
    return jnp.maximum(z, 0.0) + jnp.log(1.0 + jnp.exp(-jnp.abs(z)))


def _const_spec(shape):
    return pl.BlockSpec(shape, lambda *_: (0,) * len(shape))


def _ada_kernel(c_ref, w_ref, b_ref, o_ref):
    o_ref[...] = jnp.dot(c_ref[...], w_ref[...], preferred_element_type=F32,
                         precision=lax.Precision.HIGHEST) + b_ref[...]


def _ada(c, w, b):
    m, d = c.shape
    n = w.shape[1]
    tn = n // 6
    return pl.pallas_call(
        _ada_kernel,
        grid=(n // tn,),
        in_specs=[_const_spec((m, d)),
                  pl.BlockSpec((d, tn), lambda j: (0, j)),
                  pl.BlockSpec((1, tn), lambda j: (0, j))],
        out_specs=pl.BlockSpec((m, tn), lambda j: (0, j)),
        out_shape=jax.ShapeDtypeStruct((m, n), F32),
        compiler_params=_params("arbitrary"),
        name="ada",
    )(c, w, b.reshape(1, n))


def _qkv_prompt_kernel(x_ref, shift_ref, scale_ref, wq_ref, wkt_ref, wvt_ref, wv_ref,
                       q_ref, kt_ref, ktb_ref, vt_ref, vb_ref, *, q_scale):
    h = (_layer_norm(x_ref[...]) * (1.0 + scale_ref[0]) + shift_ref[0]).astype(BF16)
    q = jnp.dot(h, wq_ref[...], preferred_element_type=F32)
    q_ref[...] = (q * q_scale).astype(BF16)
    kt = lax.dot_general(wkt_ref[...], h, _NT, preferred_element_type=F32)
    kt_ref[0] = kt
    ktb_ref[0] = kt.astype(BF16)
    vt_ref[0] = lax.dot_general(wvt_ref[...], h, _NT, preferred_element_type=F32)
    vb_ref[...] = jnp.dot(h, wv_ref[...], preferred_element_type=F32).astype(BF16)


def _qkv_prompt(x, shift, scale, wq, wkt, wvt, wv, n_seq, tm):
    n, d = x.shape
    da = wq.shape[1]
    s_len = n // n_seq
    tps = s_len // tm
    mod = pl.BlockSpec((1, 1, d), lambda i: (i // tps, 0, 0))
    row = pl.BlockSpec((tm, da), lambda i: (i, 0))
    tr = pl.BlockSpec((1, da, tm), lambda i: (i // tps, 0, i % tps))
    return pl.pallas_call(
        functools.partial(_qkv_prompt_kernel, q_scale=HEAD_DIM ** -0.5),
        grid=(n // tm,),
        in_specs=[pl.BlockSpec((tm, d), lambda i: (i, 0)), mod, mod,
                  _const_spec((d, da)), _const_spec((da, d)), _const_spec((da, d)), _const_spec((d, da))],
        out_specs=[row, tr, tr, tr, row],
        out_shape=[jax.ShapeDtypeStruct((n, da), BF16),
                   jax.ShapeDtypeStruct((n_seq, da, s_len), F32),
                   jax.ShapeDtypeStruct((n_seq, da, s_len), BF16),
                   jax.ShapeDtypeStruct((n_seq, da, s_len), F32),
                   jax.ShapeDtypeStruct((n, da), BF16)],
        compiler_params=_params("arbitrary"),
        name="qkv_prompt",
    )(x, shift, scale, wq, wkt, wvt, wv)


def _qkv_sample_kernel(x_ref, shift_ref, scale_ref, wq_ref, wk_ref, wv_ref, q_ref, k_ref, v_ref, *, q_scale):
    h = (_layer_norm(x_ref[...]) * (1.0 + scale_ref[0]) + shift_ref[0]).astype(BF16)
    q_ref[...] = (jnp.dot(h, wq_ref[...], preferred_element_type=F32) * q_scale).astype(BF16)
    k_ref[...] = jnp.dot(h, wk_ref[...], preferred_element_type=F32)
    v_ref[...] = jnp.dot(h, wv_ref[...], preferred_element_type=F32)


def _qkv_sample(x, shift, scale, wq, wk, wv, tm):
    n, d = x.shape
    da = wq.shape[1]
    mod = pl.BlockSpec((1, tm, d), lambda i: (i, 0, 0))
    row = pl.BlockSpec((tm, da), lambda i: (i, 0))
    return pl.pallas_call(
        functools.partial(_qkv_sample_kernel, q_scale=HEAD_DIM ** -0.5),
        grid=(n // tm,),
        in_specs=[pl.BlockSpec((tm, d), lambda i: (i, 0)), mod, mod,
                  _const_spec((d, da)), _const_spec((d, da)), _const_spec((d, da))],
        out_specs=[row, row, row],
        out_shape=[jax.ShapeDtypeStruct((n, da), BF16),
                   jax.ShapeDtypeStruct((n, da), F32),
                   jax.ShapeDtypeStruct((n, da), F32)],
        compiler_params=_params("arbitrary"),
        name="qkv_sample",
    )(x, shift, scale, wq, wk, wv)


def _attn_prompt_kernel(bias_ref, q_ref, kt_ref, v_ref, u_ref, o_ref, *, tq, tk):
    g = pl.program_id(1)
    qi = pl.program_id(2)
    q2 = q_ref[...]
    lane = lax.broadcasted_iota(jnp.int32, (tq, LANES), 1)
    n_diag = tq // tk
    row = lax.broadcasted_iota(jnp.int32, (tq, tk), 0)
    col = lax.broadcasted_iota(jnp.int32, (tq, tk), 1)
    u = u_ref[...]
    heads = range(HEADS_PER_GROUP)
    qs = [jnp.where(lane // HEAD_DIM == p, q2, jnp.zeros_like(q2)) for p in heads]
    biases = [bias_ref[g * HEADS_PER_GROUP + p] for p in heads]

    def sub_block(off, state, vis):
        kt = kt_ref[0, :, pl.ds(off, tk)]
        v = v_ref[pl.ds(off, tk), :]
        new_state = []
        for qp, bias, (o, c) in zip(qs, biases, state):
            z = jnp.dot(qp, kt, preferred_element_type=F32) + bias
            if vis is not None:
                z = jnp.where(vis, z, MASKED_LOGIT)
            r = jnp.dot(_softplus(z).astype(BF16), u, preferred_element_type=F32)
            a = jnp.exp(z - r[:, :tk] - jnp.concatenate([c] * (tk // LANES), axis=1))
            o = o + jnp.dot(a.astype(BF16), v, preferred_element_type=F32)
            new_state.append((o, c + r[:, tk:]))
        return tuple(new_state)

    def key_block(kb, state, masked):
        for d in reversed(range(n_diag)):
            off = pl.multiple_of(kb * tq + d * tk, tk)
            state = sub_block(off, state, (d * tk + col < row) if masked else None)
        return state

    zeros = jnp.zeros((tq, LANES), F32)
    state = key_block(qi, tuple((zeros, zeros) for _ in heads), True)
    state = lax.fori_loop(0, qi, lambda j, st: key_block(qi - 1 - j, st, False), state)
    o_ref[...] = jnp.where(lane < HEAD_DIM, state[0][0], state[1][0]).astype(o_ref.dtype)


def _suffix_ones(n, extra=0):
    j = lax.broadcasted_iota(jnp.int32, (n, n + extra), 0)
    s = lax.broadcasted_iota(jnp.int32, (n, n + extra), 1)
    return ((j >= s) | (s >= n)).astype(BF16)


def _attn_prompt(q, ktb, vb, sb_bias, n_seq, tq, tk):
    n, da = q.shape
    s_len = n // n_seq
    nq = s_len // tq
    n_groups = da // LANES
    return pl.pallas_call(
        functools.partial(_attn_prompt_kernel, tq=tq, tk=tk),
        grid_spec=pltpu.PrefetchScalarGridSpec(
            num_scalar_prefetch=1,
            grid=(n_seq, n_groups, nq),
            in_specs=[pl.BlockSpec((tq, LANES), lambda b, g, i, *_: (b * nq + i, g)),
                      pl.BlockSpec((1, LANES, s_len), lambda b, g, i, *_: (b, g, 0)),
                      pl.BlockSpec((s_len, LANES), lambda b, g, i, *_: (b, g)),
                      pl.BlockSpec((tk, tk + LANES), lambda b, g, i, *_: (0, 0))],
            out_specs=pl.BlockSpec((tq, LANES), lambda b, g, i, *_: (b * nq + i, g)),
        ),
        out_shape=jax.ShapeDtypeStruct((n, da), BF16),
        compiler_params=_params("arbitrary", "arbitrary", "arbitrary"),
        name="attn_prompt",
    )(sb_bias, q, ktb, vb, _suffix_ones(tk, LANES))


def _attn_sample_kernel(pt_ref, qbd_ref, bias_ref, knew_ref, vnew_ref, u_ref, ck_hbm, cv_hbm,
                        o_ref, acc_ref, carry_ref, kbuf, vbuf, kpages, vpages, sems,
                        *, pages_per_step, n_pages, n_heads, t_new):
    b = pl.program_id(0)
    j = pl.program_id(1)
    n_steps = pl.num_programs(1)
    step = b * n_steps + j
    slot = step % 2
    n_rows = t_new * n_heads
    page = kbuf.shape[1]
    qbd = qbd_ref[0]
    bias = bias_ref[...]
    u = u_ref[...]

    def page_copies(seq, jj, s):
        copies = []
        for i in range(pages_per_step):
            pid = pt_ref[seq * n_pages + (n_pages - 1 - (jj * pages_per_step + i))]
            copies.append(pltpu.make_async_copy(ck_hbm.at[pid], kpages.at[s, i], sems.at[0, s]))
            copies.append(pltpu.make_async_copy(cv_hbm.at[pid], vpages.at[s, i], sems.at[1, s]))
        return copies

    @pl.when(step == 0)
    def _():
        for cp in page_copies(0, 0, 0):
            cp.start()

    @pl.when(step + 1 < pl.num_programs(0) * n_steps)
    def _():
        wrap = j + 1 == n_steps
        for cp in page_copies(jnp.where(wrap, b + 1, b), jnp.where(wrap, 0, j + 1), 1 - slot):
            cp.start()

    def blocks(kts, vts, vis, o, c):
        zs = [jnp.dot(qbd, kt.astype(BF16), preferred_element_type=F32) + bias for kt in kts]
        sps = [_softplus(z) for z in zs]
        if vis is not None:
            sps = [jnp.where(vis, sp, 0.0) for sp in sps]
        rs = [jnp.dot(sp.astype(BF16), u, preferred_element_type=F32) for sp in sps]
        for z, r, vt in zip(zs, rs, vts):
            a = jnp.exp(z - r[:, :page] - jnp.concatenate([c] * (page // LANES), axis=1))
            if vis is not None:
                a = jnp.where(vis, a, 0.0)
            o = o + lax.dot_general(a.astype(BF16), vt.astype(BF16), _NT, preferred_element_type=F32)
            c = c + r[:, page:]
        return o, c

    @pl.when(j == 0)
    def _():
        kbuf[...] = jnp.zeros(kbuf.shape, F32)
        vbuf[...] = jnp.zeros(vbuf.shape, F32)
        kbuf[:, 0:t_new] = knew_ref[0]
        vbuf[:, 0:t_new] = vnew_ref[0]
        row_t = lax.broadcasted_iota(jnp.int32, (n_rows, page), 0) // n_heads
        col = lax.broadcasted_iota(jnp.int32, (n_rows, page), 1)
        o, c = blocks([kbuf[...]], [vbuf[...]], col < row_t,
                      jnp.zeros(acc_ref.shape, F32), jnp.zeros(carry_ref.shape, F32))
        acc_ref[...] = o
        carry_ref[...] = c

    for cp in page_copies(b, j, slot):
        cp.wait()
    o, c = blocks([kpages[slot, i] for i in range(pages_per_step)],
                  [vpages[slot, i] for i in range(pages_per_step)], None, acc_ref[...], carry_ref[...])
    acc_ref[...] = o
    carry_ref[...] = c

    @pl.when(j == pl.num_programs(1) - 1)
    def _():
        r = lax.broadcasted_iota(jnp.int32, o.shape, 0)
        l = lax.broadcasted_iota(jnp.int32, o.shape, 1)
        own = jnp.where(l // HEAD_DIM == r % n_heads, o, 0.0)
        o_ref[0] = own.reshape(t_new, n_heads, o.shape[1]).sum(axis=1)


def _attn_sample(q, k_new, v_new, sb_bias, cache_kt, cache_vt, page_table, t_new, pages_per_step):
    n, da = q.shape
    n_seq = n // t_new
    n_heads = da // HEAD_DIM
    n_pages = page_table.shape[1]
    page = cache_kt.shape[2]
    n_rows = t_new * n_heads
    assert n_pages % pages_per_step == 0 and t_new <= page
    head_of_col = jnp.arange(da, dtype=jnp.int32) // HEAD_DIM
    blockdiag = head_of_col[None, :] == jnp.arange(n_heads, dtype=jnp.int32)[:, None]
    qbd = jnp.where(blockdiag[None, None], q.reshape(n_seq, t_new, 1, da), jnp.zeros((), q.dtype))
    qbd = qbd.reshape(n_seq, n_rows, da)
    assert page % LANES == 0
    bias_rows = jnp.broadcast_to(jnp.tile(sb_bias.astype(F32), t_new).reshape(n_rows, 1), (n_rows, page))
    knew_t = k_new.reshape(n_seq, t_new, da).transpose(0, 2, 1)
    vnew_t = v_new.reshape(n_seq, t_new, da).transpose(0, 2, 1)

    seq3 = lambda b, j, pt: (b, 0, 0)
    const2 = lambda b, j, pt: (0, 0)
    specs = [pl.BlockSpec((1, n_rows, da), seq3),
             pl.BlockSpec((n_rows, page), const2),
             pl.BlockSpec((1, da, t_new), seq3),
             pl.BlockSpec((1, da, t_new), seq3),
             pl.BlockSpec((page, page + LANES), const2),
             pl.BlockSpec(memory_space=pl.ANY),
             pl.BlockSpec(memory_space=pl.ANY)]
    page_buf = pltpu.VMEM((2, pages_per_step, da, page), F32)
    out = pl.pallas_call(
        functools.partial(_attn_sample_kernel, pages_per_step=pages_per_step, n_pages=n_pages,
                          n_heads=n_heads, t_new=t_new),
        grid_spec=pltpu.PrefetchScalarGridSpec(
            num_scalar_prefetch=1,
            grid=(n_seq, n_pages // pages_per_step),
            in_specs=specs,
            out_specs=pl.BlockSpec((1, t_new, da), seq3),
            scratch_shapes=[pltpu.VMEM((n_rows, da), F32), pltpu.VMEM((n_rows, LANES), F32),
                            pltpu.VMEM((da, page), F32), pltpu.VMEM((da, page), F32),
                            page_buf, page_buf, pltpu.SemaphoreType.DMA((2, 2))],
        ),
        out_shape=jax.ShapeDtypeStruct((n_seq, t_new, da), F32),
        compiler_params=_params("arbitrary", "arbitrary"),
        name="attn_sample",
    )(page_table.reshape(-1), qbd, bias_rows, knew_t, vnew_t, _suffix_ones(page, LANES), cache_kt, cache_vt)
    return out.reshape(n, da)


def _post_kernel(i, *refs, prompt, n_sub, tiles_per_seq, t_new, alpha, d_conv, n_experts):
    if prompt:
        (x_ref, attn_ref, sh1, sc1, g1, sh2, sc2, cnt0_ref, lt_ref, wr_ref, wc_ref, wpa_ref, wpb_ref, wo_ref,
         ln1g_ref, ln1b_ref, wrt_ref, br_ref,
         x1_ref, h2_ref, idx_ref, p_ref, rank_ref, cnt_out_ref, cu_ref, cnt_ref, tail_ref) = refs
    else:
        (x_ref, attn_ref, sh1, sc1, g1, sh2, sc2, cnt0_ref, lt_ref, p1_ref, p2_ref,
         wr_ref, wc_ref, wpa_ref, wpb_ref, wo_ref, ln1g_ref, ln1b_ref, wrt_ref, br_ref,
         x1_ref, h2_ref, idx_ref, p_ref, rank_ref, cnt_out_ref, cu_ref, cnt_ref) = refs
    tm, d = x_ref.shape
    hm = tm // n_sub
    dc = d_conv
    subs = range(n_sub)
    rows = [pl.ds(h * hm, hm) for h in subs]

    def mod(m_ref):
        return [m_ref[0] for _ in subs] if prompt else [m_ref[0, r, :] for r in rows]

    @pl.when(i == 0)
    def _():
        cnt_ref[...] = cnt0_ref[...]

    xs = [x_ref[r, :] for r in rows]
    hs = [(_layer_norm(x) * (1.0 + sc) + sh).astype(BF16) for x, sc, sh in zip(xs, mod(sc1), mod(sh1))]
    ys = [jnp.dot(h, wr_ref[...], preferred_element_type=F32) for h in hs]
    cus = [y[:, 2 * dc:3 * dc] * y[:, 0:dc] for y in ys]
    wc = wc_ref[...]
    convs = []
    for h in subs:
        cu = cus[h]
        r1 = pltpu.roll(cu, 1, 0)
        r2 = pltpu.roll(cu, 2, 0)
        if prompt:
            if h == 0:
                tail = jnp.where((i % tiles_per_seq) == 0, 0.0, tail_ref[...])
            else:
                tail = cus[h - 1][hm - SUBLANES:, :]
            row8 = lax.broadcasted_iota(jnp.int32, (SUBLANES, dc), 0)
            f1 = jnp.where(row8 >= 1, r1[0:SUBLANES], pltpu.roll(tail, 1, 0))
            f2 = jnp.where(row8 >= 2, r2[0:SUBLANES], pltpu.roll(tail, 2, 0))
            z1 = jnp.concatenate([f1, r1[SUBLANES:]], axis=0)
            z2 = jnp.concatenate([f2, r2[SUBLANES:]], axis=0)
        else:
            tpos = lax.broadcasted_iota(jnp.int32, (hm, dc), 0) % t_new
            z1 = jnp.where(tpos >= 1, r1, p1_ref[rows[h], :])
            z2 = jnp.where(tpos >= 2, r2, p2_ref[rows[h], :])
            cu_ref[rows[h], :] = cu
        convs.append(wc[0:1] * z2 + wc[1:2] * z1 + wc[2:3] * cu)
    if prompt:
        tail_ref[...] = cus[-1][hm - SUBLANES:, :]
        cu_ref[0] = cus[-1][hm - SUBLANES:, :]
    branch_bs = [jnp.dot((y[:, dc:2 * dc] * conv).astype(BF16), wpb_ref[...], preferred_element_type=F32)
                 for y, conv in zip(ys, convs)]
    branch_as = [jnp.dot(attn_ref[r, :], wpa_ref[...], preferred_element_type=F32) for r in rows]
    mergeds = [jax.nn.sigmoid(y[:, 3 * dc:3 * dc + d]) * ba + jax.nn.sigmoid(y[:, 3 * dc + d:]) * bb
               for y, ba, bb in zip(ys, branch_as, branch_bs)]
    yos = [jnp.dot(m.astype(BF16), wo_ref[...], preferred_element_type=F32) for m in mergeds]
    x1s = [_layer_norm(alpha * x + (1.0 + g) * yo) * ln1g_ref[...] + ln1b_ref[...]
           for x, g, yo in zip(xs, mod(g1), yos)]
    h2s = [_layer_norm(x1) * (1.0 + sc) + sh for x1, sc, sh in zip(x1s, mod(sc2), mod(sh2))]
    logits = [lax.dot_general(h2, wrt_ref[...], _NT, preferred_element_type=F32,
                              precision=lax.Precision.HIGHEST) + br_ref[...] for h2 in h2s]
    for h in subs:
        x1_ref[rows[h], :] = x1s[h]
        _rows_to_tiles(h2_ref.at[pl.ds(h * hm * SUBLANES, hm * SUBLANES)], h2s[h])

    lane_e = lax.broadcasted_iota(jnp.int32, (hm, n_experts), 1)
    lane_o = lax.broadcasted_iota(jnp.int32, (hm, LANES), 1)
    idx_outs = [jnp.zeros((hm, LANES), jnp.int32) for _ in subs]
    e_outs = [jnp.zeros((hm, LANES), F32) for _ in subs]
    denoms = [jnp.zeros((hm, 1), F32) for _ in subs]
    tops = [None for _ in subs]
    sels = [[] for _ in subs]
    for k in range(TOP_K):
        for h in subs:
            m = jnp.max(logits[h], axis=-1, keepdims=True)
            sel = jnp.min(jnp.where(logits[h] == m, lane_e, n_experts), axis=-1, keepdims=True)
            logits[h] = jnp.where(lane_e == sel, -jnp.inf, logits[h])
            if k == 0:
                tops[h] = m
            e = jnp.exp(m - tops[h])
            denoms[h] = denoms[h] + e
            idx_outs[h] = jnp.where(lane_o == k, sel, idx_outs[h])
            e_outs[h] = jnp.where(lane_o == k, e, e_outs[h])
            sels[h].append(sel)
    for h in subs:
        idx_ref[rows[h], :] = idx_outs[h]
        p_ref[rows[h], :] = e_outs[h] / denoms[h]

    base = cnt_ref[0:1, :]
    for h in subs:
        onehot = jnp.zeros((hm, LANES), F32)
        for sel in sels[h]:
            onehot = onehot + (lane_o == sel).astype(F32)
        before = jnp.dot(lt_ref[...], onehot.astype(BF16), preferred_element_type=F32) + base
        rank_out = jnp.zeros((hm, LANES), jnp.int32)
        for k, sel in enumerate(sels[h]):
            rk = jnp.sum(jnp.where(lane_o == sel, before, 0.0), axis=-1, keepdims=True)
            rank_out = jnp.where(lane_o == k, rk.astype(jnp.int32), rank_out)
        rank_ref[rows[h], :] = rank_out
        base = base + jnp.sum(onehot, axis=0, keepdims=True)
    cnt = jnp.broadcast_to(base, cnt_ref.shape)
    cnt_ref[...] = cnt
    cnt_out_ref[...] = cnt


def _post(x, attn, mods, conv_hist, counts0, w, *, prompt, n_seq, t_new, tm, n_sub, alpha):
    n, d = x.shape
    da = attn.shape[1]
    dc = w["w_conv"].shape[1]
    n_exp = w["w_router_t"].shape[0]
    assert n_exp <= LANES and tm % n_sub == 0
    hm = tm // n_sub
    tiles_per_seq = (n // n_seq) // tm if prompt else 0
    if prompt:
        mod = pl.BlockSpec((1, 1, d), lambda i: (i // tiles_per_seq, 0, 0))
    else:
        mod = pl.BlockSpec((1, tm, d), lambda i: (i, 0, 0))
    rows = lambda c: pl.BlockSpec((tm, c), lambda i: (i, 0))
    t_row = lax.broadcasted_iota(jnp.int32, (hm, hm), 0)
    t_col = lax.broadcasted_iota(jnp.int32, (hm, hm), 1)
    strictly_lower = (t_col < t_row).astype(BF16)
    in_specs = [rows(d), rows(da)] + [mod] * 5 + [_const_spec((SUBLANES, LANES)), _const_spec((hm, hm))]
    args = [x, attn, *mods, counts0, strictly_lower]
    if not prompt:
        in_specs += [rows(dc), rows(dc)]
        args += list(conv_hist)
    weights = [w["w_rest"], w["w_conv"], w["w_pa"], w["w_pb"], w["w_o"], w["ln1_g"], w["ln1_b"],
               w["w_router_t"], w["b_router"]]
    in_specs += [_const_spec(a.shape) for a in weights]
    args += weights
    assert d == SUBLANES * LANES
    out_specs = [rows(d), pl.BlockSpec((tm * SUBLANES, LANES), lambda i: (i, 0)),
                 rows(LANES), rows(LANES), rows(LANES), _const_spec((SUBLANES, LANES))]
    out_shape = [jax.ShapeDtypeStruct((n, d), F32),
                 jax.ShapeDtypeStruct((n * SUBLANES, LANES), F32),
                 jax.ShapeDtypeStruct((n, LANES), jnp.int32),
                 jax.ShapeDtypeStruct((n, LANES), F32),
                 jax.ShapeDtypeStruct((n, LANES), jnp.int32),
                 jax.ShapeDtypeStruct((SUBLANES, LANES), F32)]
    scratch = [pltpu.VMEM((SUBLANES, LANES), F32)]
    if prompt:
        out_specs.append(pl.BlockSpec((1, SUBLANES, dc), lambda i: (i // tiles_per_seq, 0, 0)))
        out_shape.append(jax.ShapeDtypeStruct((n_seq, SUBLANES, dc), F32))
        scratch.append(pltpu.VMEM((SUBLANES, dc), F32))
    else:
        out_specs.append(rows(dc))
        out_shape.append(jax.ShapeDtypeStruct((n, dc), F32))

    def body(*refs):
        _post_kernel(pl.program_id(0), *refs, prompt=prompt, n_sub=n_sub, tiles_per_seq=tiles_per_seq,
                     t_new=t_new, alpha=alpha, d_conv=dc, n_experts=n_exp)

    return pl.pallas_call(
        body,
        grid=(n // tm,),
        in_specs=in_specs,
        out_specs=out_specs,
        out_shape=out_shape,
        scratch_shapes=scratch,
        compiler_params=_params("arbitrary"),
        name="post_prompt" if prompt else "post_sample",
    )(*args)


def _count_le(sorted_vals, x):
    return jnp.sum((sorted_vals[None, :] <= x[:, None]).astype(jnp.int32), axis=1)


def _lookup(table, idx):
    hit = idx[:, None] == jnp.arange(table.shape[0], dtype=jnp.int32)[None, :]
    return jnp.sum(jnp.where(hit, table[None, :], 0), axis=1)


def _moe_plan(counts, n_assign):
    n_exp = counts.shape[0]
    assert n_assign % MOE_BLOCK == 0
    n_tiles = n_assign // MOE_BLOCK
    end = jnp.cumsum(counts)
    start = end - counts
    tile_lo = jnp.arange(n_tiles, dtype=jnp.int32) * MOE_BLOCK
    e_lo = jnp.minimum(_count_le(end, tile_lo), n_exp - 1)
    e_hi = jnp.minimum(_count_le(end, tile_lo + (MOE_BLOCK - 1)), n_exp - 1)
    n_vis = e_hi - e_lo + 1
    v_end = jnp.cumsum(n_vis)
    v_start = v_end - n_vis
    n_visits = n_tiles + n_exp - 1
    v = jnp.arange(n_visits, dtype=jnp.int32)
    used = v < v_end[-1]
    tile_v = jnp.minimum(_count_le(v_end, v), n_tiles - 1)
    first_v = used & (v == _lookup(v_start, tile_v))
    expert_v = jnp.where(used, _lookup(e_lo, tile_v) + v - _lookup(v_start, tile_v), e_hi[-1])
    base = tile_v * MOE_BLOCK
    lo_v = jnp.where(used, jnp.clip(_lookup(start, expert_v) - base, 0, MOE_BLOCK), 0)
    hi_v = jnp.where(used, jnp.clip(_lookup(end, expert_v) - base, 0, MOE_BLOCK), 0)
    as_i32 = lambda a: a.astype(jnp.int32)
    return as_i32(start), as_i32(tile_v), as_i32(expert_v), as_i32(lo_v), as_i32(hi_v), as_i32(first_v)


def _rows_to_tiles(ref, x):
    m, d = x.shape
    per = d // LANES
    for s in range(per):
        ref[pl.ds(s, m, stride=per), :] = x[:, s * LANES:(s + 1) * LANES]


def _tiles_to_rows(ref, m, lead=()):
    per = ref.shape[-2] // m
    return jnp.concatenate([ref[(*lead, pl.ds(s, m, stride=per), slice(None))] for s in range(per)], axis=1)


def _for_assignments(tm, fn):
    def row(r, carry):
        for k in range(TOP_K):
            fn(r, k, r * TOP_K + k)
        return carry
    lax.fori_loop(0, tm, row, 0, unroll=2)


def _tile_of(ref, t, lead=()):
    return ref.at[(*lead, pl.ds(pl.multiple_of(t, SUBLANES), SUBLANES))]


def _dispatch_kernel(dst_ref, hp_ref, hs_ref, xb_hbm, sem, *, n_tiles_p, tm):
    i = pl.program_id(0)

    def send_tile(h_ref):
        def copy(r, k, a):
            return pltpu.make_async_copy(_tile_of(h_ref, r * SUBLANES), _tile_of(xb_hbm, dst_ref[a]), sem)
        _for_assignments(tm, lambda r, k, a: copy(r, k, a).start(priority=k % 2))
        _for_assignments(tm, lambda r, k, a: copy(r, k, a).wait())

    @pl.when(i < n_tiles_p)
    def _():
        send_tile(hp_ref)

    @pl.when(i >= n_tiles_p)
    def _():
        send_tile(hs_ref)


def _dispatch(h2_p, h2_s, dst_rows, tm):
    per = SUBLANES
    n_tiles_p = h2_p.shape[0] // (tm * per)
    n_tiles_s = h2_s.shape[0] // (tm * per)
    flat = pl.BlockSpec((tm * TOP_K,), lambda i: (i,), memory_space=pltpu.SMEM)
    return pl.pallas_call(
        functools.partial(_dispatch_kernel, n_tiles_p=n_tiles_p, tm=tm),
        grid=(n_tiles_p + n_tiles_s,),
        in_specs=[flat,
                  pl.BlockSpec((tm * per, LANES), lambda i: (jnp.minimum(i, n_tiles_p - 1), 0)),
                  pl.BlockSpec((tm * per, LANES), lambda i: (jnp.maximum(i - n_tiles_p, 0), 0))],
        out_specs=pl.BlockSpec(memory_space=pl.ANY),
        scratch_shapes=[pltpu.SemaphoreType.DMA(())],
        out_shape=jax.ShapeDtypeStruct((dst_rows.shape[0] * per, LANES), F32),
        compiler_params=_params("arbitrary"),
        name="dispatch",
    )(dst_rows, h2_p, h2_s)


def _moe_kernel(tile_ref, exp_ref, lo_ref, hi_ref, first_ref, x_ref, wg_ref, bg_ref, wu_ref, bu_ref,
                wd_ref, bd_ref, o_ref, wg_bf, wu_bf, wd_bf):
    v = pl.program_id(0)
    lo = lo_ref[v]
    hi = hi_ref[v]

    @pl.when((v == 0) | (exp_ref[v] != exp_ref[jnp.maximum(v - 1, 0)]))
    def _():
        wg_bf[...] = wg_ref[0].astype(BF16)
        wu_bf[...] = wu_ref[0].astype(BF16)
        wd_bf[...] = wd_ref[0].astype(BF16)

    @pl.when(hi > lo)
    def _():
        x = _tiles_to_rows(x_ref, MOE_BLOCK).astype(BF16)
        g = jnp.minimum(jnp.dot(x, wg_bf[...], preferred_element_type=F32) + bg_ref[0], SWIGLU_LIMIT)
        u = jnp.clip(jnp.dot(x, wu_bf[...], preferred_element_type=F32) + bu_ref[0], -SWIGLU_LIMIT, SWIGLU_LIMIT)
        act = g * jax.nn.sigmoid(SWIGLU_ALPHA * g) * (u + 1.0)
        y = jnp.dot(act.astype(BF16), wd_bf[...], preferred_element_type=F32) + bd_ref[0]
        row = lax.broadcasted_iota(jnp.int32, y.shape, 0)
        y = jnp.where((row >= lo) & (row < hi), y, 0.0)

        @pl.when(first_ref[v] == 1)
        def _():
            _rows_to_tiles(o_ref, y)

        @pl.when(first_ref[v] == 0)
        def _():
            _rows_to_tiles(o_ref, y + _tiles_to_rows(o_ref, MOE_BLOCK))


def _moe(xb, plan, wg, bg, wu, bu, wd, bd):
    _, tile_v, expert_v, lo_v, hi_v, first_v = plan
    n_exp, d, de = wg.shape
    assert d == SUBLANES * LANES
    tile = pl.BlockSpec((MOE_BLOCK * SUBLANES, LANES), lambda v, t, e, *_: (t[v], 0))
    wspec = lambda a, b: pl.BlockSpec((1, a, b), lambda v, t, e, *_: (e[v], 0, 0))
    return pl.pallas_call(
        _moe_kernel,
        grid_spec=pltpu.PrefetchScalarGridSpec(
            num_scalar_prefetch=5,
            grid=(tile_v.shape[0],),
            in_specs=[tile, wspec(d, de), wspec(1, de), wspec(d, de), wspec(1, de), wspec(de, d), wspec(1, d)],
            out_specs=tile,
            scratch_shapes=[pltpu.VMEM((d, de), BF16), pltpu.VMEM((d, de), BF16), pltpu.VMEM((de, d), BF16)],
        ),
        out_shape=jax.ShapeDtypeStruct(xb.shape, F32),
        compiler_params=_params("arbitrary"),
        name="moe",
    )(tile_v, expert_v, lo_v, hi_v, first_v, xb, wg, bg.reshape(n_exp, 1, de), wu, bu.reshape(n_exp, 1, de),
      wd, bd.reshape(n_exp, 1, d))


def _final_kernel(src_ref, src_next_ref, x1_ref, p_ref, g2_ref, lng_ref, lnb_ref, yb_hbm, o_ref, ybuf, sems,
                  *, alpha, tm):
    i = pl.program_id(0)
    slot = i % 2

    def copy(src, s, r, k):
        return pltpu.make_async_copy(_tile_of(yb_hbm, src), _tile_of(ybuf, r * SUBLANES, (s, k)), sems.at[s])

    def fetch(src_r, s):
        _for_assignments(tm, lambda r, k, a: copy(src_r[a], s, r, k).start(priority=k % 2))

    @pl.when(i == 0)
    def _():
        fetch(src_ref, 0)

    @pl.when(i + 1 < pl.num_programs(0))
    def _():
        fetch(src_next_ref, 1 - slot)

    _for_assignments(tm, lambda r, k, a: copy(0, slot, r, k).wait())
    p = p_ref[...]
    ys = [_tiles_to_rows(ybuf, tm, (slot, k)) for k in range(TOP_K)]
    y = (p[:, 0:1] * ys[0] + p[:, 1:2] * ys[1]) + (p[:, 2:3] * ys[2] + p[:, 3:4] * ys[3])
    o_ref[...] = _layer_norm(alpha * x1_ref[...] + (1.0 + g2_ref[0]) * y) * lng_ref[...] + lnb_ref[...]


def _final(x1, probs, gate2, yb, src_rows, ln_g, ln_b, *, prompt, n_seq, tm, alpha, row_offset):
    n, d = x1.shape
    n_tiles = n // tm
    assert row_offset % tm == 0
    off = row_offset // tm
    if prompt:
        tiles_per_seq = (n // n_seq) // tm
        mod = pl.BlockSpec((1, 1, d), lambda i: (i // tiles_per_seq, 0, 0))
    else:
        mod = pl.BlockSpec((1, tm, d), lambda i: (i, 0, 0))
    rows = lambda c: pl.BlockSpec((tm, c), lambda i: (i, 0))
    flat = pl.BlockSpec((tm * TOP_K,), lambda i: (off + i,), memory_space=pltpu.SMEM)
    flat_next = pl.BlockSpec((tm * TOP_K,), lambda i: (off + jnp.minimum(i + 1, n_tiles - 1),),
                             memory_space=pltpu.SMEM)
    return pl.pallas_call(
        functools.partial(_final_kernel, alpha=alpha, tm=tm),
        grid=(n_tiles,),
        in_specs=[flat, flat_next, rows(d), rows(LANES), mod, _const_spec((1, d)), _const_spec((1, d)),
                  pl.BlockSpec(memory_space=pl.ANY)],
        out_specs=rows(d),
        scratch_shapes=[pltpu.VMEM((2, TOP_K, tm * SUBLANES, LANES), F32), pltpu.SemaphoreType.DMA((2,))],
        out_shape=jax.ShapeDtypeStruct((n, d), F32),
        compiler_params=_params("arbitrary"),
        name="final_prompt" if prompt else "final_sample",
    )(src_rows, src_rows, x1, probs, gate2, ln_g, ln_b, yb)


def _pick(n, candidates):
    for c in candidates:
        if n % c == 0:
            return c
    raise ValueError(f"no tile size in {candidates} divides {n}")


def _decoder_layer(xp, xs, c_all, cache_kt, cache_vt, conv_state, page_table, lw, *, n_p, n_s, t_new, depth):
    d = xp.shape[1]
    s_len = xp.shape[0] // n_p
    n_rows_s = xs.shape[0]
    alpha = (2 * depth) ** 0.25
    d_attn = lw["w_pa"].shape[0]
    d_conv = lw["w_conv"].shape[1]
    n_exp = lw["w_router"].shape[1]

    ada = _ada(c_all, lw["w_ada"], lw["b_ada"])
    mods = [ada[:, k * d:(k + 1) * d] for k in range(6)]
    tm_s = _pick(n_rows_s, (256, 128, 64, 32, 16, 8))
    mods_p = [m[:n_p].reshape(n_p, 1, d) for m in mods]
    mods_s = [jnp.repeat(m[n_p:], t_new, axis=0).reshape(n_rows_s // tm_s, tm_s, d) for m in mods]

    w_in = lw["w_in"]
    wq = w_in[:, :d_attn].astype(BF16)
    wk = w_in[:, d_attn:2 * d_attn].astype(BF16)
    wv = w_in[:, 2 * d_attn:3 * d_attn].astype(BF16)
    weights = {
        "w_rest": w_in[:, 3 * d_attn:].astype(BF16),
        "w_conv": lw["w_conv"],
        "w_pa": lw["w_pa"].astype(BF16), "w_pb": lw["w_pb"].astype(BF16), "w_o": lw["w_o"].astype(BF16),
        "ln1_g": lw["ln1_g"].reshape(1, d), "ln1_b": lw["ln1_b"].reshape(1, d),
        "w_router_t": lw["w_router"].T, "b_router": lw["b_router"].reshape(1, n_exp),
    }

    tm_p = _pick(s_len, (512, 256, 128))
    q_p, kt_p, ktb_p, vt_p, vb_p = _qkv_prompt(xp, mods_p[0], mods_p[1], wq, wk.T, wv.T, wv, n_p, tm_p)
    tq = _pick(s_len, (512, 256))
    attn_p = _attn_prompt(q_p, ktb_p, vb_p, lw["sb_bias"], n_p, tq, 256)

    q_s, k_s, v_s = _qkv_sample(xs, mods_s[0], mods_s[1], wq, wk, wv, tm_s)
    pages_per_step = _pick(page_table.shape[1], (16, 8, 4, 2, 1))
    attn_s = _attn_sample(q_s, k_s, v_s, lw["sb_bias"], cache_kt, cache_vt, page_table, t_new,
                          pages_per_step).astype(BF16)

    n_tok_p = xp.shape[0]
    tm_post = tm_s
    assert s_len % (4 * tm_post) == 0
    x1_p, h2_p, idx_p, pr_p, rank_p, counts_p, tail_p = _post(
        xp, attn_p, mods_p[:5], None, jnp.zeros((SUBLANES, LANES), F32), weights,
        prompt=True, n_seq=n_p, t_new=t_new, tm=4 * tm_post, n_sub=4, alpha=alpha)
    tpos = jnp.arange(n_rows_s, dtype=jnp.int32) % t_new
    prev = conv_state
    prev_rep = jnp.repeat(prev, t_new, axis=0)
    p1 = jnp.where((tpos == 0)[:, None], prev_rep[:, 1], 0.0)
    p2 = jnp.where((tpos == 0)[:, None], prev_rep[:, 0], jnp.where((tpos == 1)[:, None], prev_rep[:, 1], 0.0))
    x1_s, h2_s, idx_s, pr_s, rank_s, counts_all, cu_s = _post(
        xs, attn_s, mods_s[:5], (p1, p2), counts_p, weights,
        prompt=False, n_seq=n_s, t_new=t_new, tm=tm_s, n_sub=1, alpha=alpha)

    idx_flat = jnp.concatenate([idx_p[:, :TOP_K], idx_s[:, :TOP_K]], axis=0).reshape(-1)
    rank_flat = jnp.concatenate([rank_p[:, :TOP_K], rank_s[:, :TOP_K]], axis=0).reshape(-1)
    counts = counts_all[0, :n_exp].astype(jnp.int32)
    plan = _moe_plan(counts, idx_flat.shape[0])
    sorted_rows = (_lookup(plan[0], idx_flat) + rank_flat) * SUBLANES
    xb = _dispatch(h2_p, h2_s, sorted_rows, tm_post)
    yb = _moe(xb, plan, lw["w_gate"], lw["b_gate"], lw["w_up"], lw["b_up"], lw["w_down"], lw["b_down"])

    ln2_g = lw["ln2_g"].reshape(1, d)
    ln2_b = lw["ln2_b"].reshape(1, d)
    out_p = _final(x1_p, pr_p, mods_p[5], yb, sorted_rows, ln2_g, ln2_b,
                   prompt=True, n_seq=n_p, tm=tm_post, alpha=alpha, row_offset=0)
    out_s = _final(x1_s, pr_s, mods_s[5], yb, sorted_rows, ln2_g, ln2_b,
                   prompt=False, n_seq=n_s, tm=tm_s, alpha=alpha, row_offset=n_tok_p)

    n_heads = d_attn // HEAD_DIM
    k_prompt = kt_p.reshape(n_p, n_heads, HEAD_DIM, s_len).transpose(0, 3, 1, 2)
    v_prompt = vt_p.reshape(n_p, n_heads, HEAD_DIM, s_len).transpose(0, 3, 1, 2)
    conv_prompt = tail_p[:, SUBLANES - 2:, :]
    k_sample = k_s.reshape(n_s, t_new, n_heads, HEAD_DIM)
    v_sample = v_s.reshape(n_s, t_new, n_heads, HEAD_DIM)
    conv_sample = cu_s.reshape(n_s, t_new, d_conv)[:, t_new - 2:, :]
    return out_p, out_s, k_prompt, v_prompt, conv_prompt, k_sample, v_sample, conv_sample


def kernel(x_prompt, x_sample, cache_k, cache_v, state_conv, page_table, c_prompt, c_sample, w_ada, b_ada, w_in, sb_bias, w_conv, w_pa, w_pb, w_o, ln1_g, ln1_b, w_router, b_router, w_gate, b_gate, w_up, b_up, w_down, b_down, ln2_g, ln2_b):
    depth = w_in.shape[0]
    n_p, s_len, d = x_prompt.shape
    n_s, t_new, _ = x_sample.shape
    assert t_new >= 2 and state_conv.shape[2] == 2
    n_pool, page, n_heads, head_dim = cache_k.shape[1:]
    assert head_dim == HEAD_DIM
    names = ("w_ada", "b_ada", "w_in", "sb_bias", "w_conv", "w_pa", "w_pb", "w_o", "ln1_g", "ln1_b",
             "w_router", "b_router", "w_gate", "b_gate", "w_up", "b_up", "w_down", "b_down", "ln2_g", "ln2_b")
    stacked = (w_ada, b_ada, w_in, sb_bias, w_conv, w_pa, w_pb, w_o, ln1_g, ln1_b,
               w_router, b_router, w_gate, b_gate, w_up, b_up, w_down, b_down, ln2_g, ln2_b)
    xp = x_prompt.reshape(n_p * s_len, d)
    xs = x_sample.reshape(n_s * t_new, d)
    c_all = jnp.concatenate([c_prompt, c_sample], axis=0)
    outs = [[] for _ in range(6)]
    for l in range(depth):
        lw = {n: a[l] for n, a in zip(names, stacked)}
        cache_kt = cache_k[l].transpose(0, 2, 3, 1).reshape(n_pool, n_heads * head_dim, page)
        cache_vt = cache_v[l].transpose(0, 2, 3, 1).reshape(n_pool, n_heads * head_dim, page)
        xp, xs, *rest = _decoder_layer(xp, xs, c_all, cache_kt, cache_vt, state_conv[l], page_table, lw,
                                       n_p=n_p, n_s=n_s, t_new=t_new, depth=depth)
        for o, r in zip(outs, rest):
            o.append(r)
    stack = [jnp.stack(o) for o in outs]
    return (xp.reshape(n_p, s_len, d), xs.reshape(n_s, t_new, d), *stack)
```
